```python
import jax, jax.numpy as jnp
from jax import lax
import numpy as np

D_MODEL = 1024
BATCH = 8
SEQ = 4096
DEPTH = 2
DEC_BATCH = 32
DEC_SEQ = 1
PAST_LEN = 16384
PAGE_SIZE = 128

D_MIX = D_MODEL
SB_HEADS = 8
SB_HEAD_DIM = (D_MIX // 2) // SB_HEADS
SB_WIDTH = SB_HEADS * SB_HEAD_DIM
SB_SCALE = SB_HEAD_DIM ** -0.5
Q_BLOCK = 128
GDN_HEADS = 4
GDN_HEAD_DIM = (D_MIX // 2) // GDN_HEADS
GDN_WIDTH = GDN_HEADS * GDN_HEAD_DIM
GDN_CONV_DIM = 3 * GDN_WIDTH
CONV_W = 4
GDN_CHUNK = 64
D_IN = 4 * SB_WIDTH + GDN_CONV_DIM + GDN_WIDTH + 2 * GDN_HEADS
SPLITS = [SB_WIDTH, 2 * SB_WIDTH, 3 * SB_WIDTH, 4 * SB_WIDTH,
          4 * SB_WIDTH + GDN_CONV_DIM, 4 * SB_WIDTH + GDN_CONV_DIM + GDN_WIDTH,
          4 * SB_WIDTH + GDN_CONV_DIM + GDN_WIDTH + GDN_HEADS]
DN_ALPHA = (2 * DEPTH) ** 0.25
DN_BETA = (8 * DEPTH) ** -0.25
SB_BIAS_LO = -8.0
SB_BIAS_HI = -5.0
LN_EPS = 1e-5
RMS_EPS = 1e-6
L2_EPS = 1e-6

kernel_name = "hybrid_stickbreak_gdn_decode_step"

F32 = jnp.float32


def _layernorm(x, g, b):
    x32 = x.astype(F32)
    mu = jnp.mean(x32, -1, keepdims=True)
    xc = x32 - mu
    var = jnp.mean(xc * xc, -1, keepdims=True)
    return (xc * lax.rsqrt(var + LN_EPS) * g.astype(F32) + b.astype(F32)).astype(x.dtype)


def _rmsnorm(x, w):
    x32 = x.astype(F32)
    return x32 * lax.rsqrt(jnp.mean(x32 * x32, -1, keepdims=True) + RMS_EPS) * w.astype(F32)


def _l2norm(x):
    return x * lax.rsqrt(jnp.sum(x * x, -1, keepdims=True) + L2_EPS)


def _mixer_inputs(x, c, w_mod_l, b_mod_l, w_in_l):
    mod = jax.nn.silu(c) @ w_mod_l + b_mod_l
    shift, scale, gate = jnp.split(mod[:, None, :], 3, axis=-1)
    h = (x * (1 + scale) + shift) @ w_in_l
    return gate, jnp.split(h, SPLITS, axis=-1)


def _sb_weights(z, mask):
    log_beta = jax.nn.log_sigmoid(z)
    log_rest = jnp.where(mask, jax.nn.log_sigmoid(-z), 0.0)
    between = lax.cumsum(log_rest, axis=3, reverse=True) - log_rest
    return jnp.where(mask, jnp.exp(log_beta + between), 0.0)


def _sb_prompt(q, k, v, bias):
    B, S, H, dh = q.shape
    nb = S // Q_BLOCK
    kf, vf = k.astype(F32), v.astype(F32)
    bias32 = bias.astype(F32)[None, :, None, None]
    qb = jnp.moveaxis(q.astype(F32).reshape(B, nb, Q_BLOCK, H, dh), 1, 0)
    k_pos = jnp.arange(S)

    def block(args):
        q_blk, b_idx = args
        z = jnp.einsum('bqhd,bkhd->bhqk', q_blk, kf) * SB_SCALE + bias32
        q_pos = b_idx * Q_BLOCK + jnp.arange(Q_BLOCK)
        w = _sb_weights(z, k_pos[None, :] < q_pos[:, None])
        return jnp.einsum('bhqk,bkhd->bqhd', w, vf)

    o = lax.map(block, (qb, jnp.arange(nb)))
    return jnp.moveaxis(o, 0, 1).reshape(B, S, H, dh)


def _sb_sample(q, k_new, v_new, k_past, v_past, bias):
    T = q.shape[1]
    P = k_past.shape[1]
    q32 = q.astype(F32)
    z = jnp.concatenate([jnp.einsum('bqhd,bkhd->bhqk', q32, k_past.astype(F32)),
                         jnp.einsum('bqhd,bkhd->bhqk', q32, k_new.astype(F32))], axis=-1) * SB_SCALE
    z = z + bias.astype(F32)[None, :, None, None]
    q_pos = P + jnp.arange(T)
    k_pos = jnp.arange(P + T)
    w = _sb_weights(z, k_pos[None, :] < q_pos[:, None])
    return (jnp.einsum('bhqk,bkhd->bqhd', w[..., :P], v_past.astype(F32))
            + jnp.einsum('bhqk,bkhd->bqhd', w[..., P:], v_new.astype(F32)))


def _causal_conv(x_pre, buf, w):
    T = x_pre.shape[1]
    xp = jnp.concatenate([buf.astype(x_pre.dtype), x_pre], axis=1)
    y = xp[:, 0:T] * w[0]
    for i in range(1, CONV_W):
        y = y + xp[:, i:i + T] * w[i]
    return jax.nn.silu(y), xp[:, xp.shape[1] - (CONV_W - 1):]


def _gdn_inputs(qkv, a, b, a_log_l, dt_bias_l):
    B, T, _ = qkv.shape
    q, k, v = jnp.split(qkv.astype(F32), 3, axis=-1)
    q = _l2norm(q.reshape(B, T, GDN_HEADS, GDN_HEAD_DIM)) * GDN_HEAD_DIM ** -0.5
    k = _l2norm(k.reshape(B, T, GDN_HEADS, GDN_HEAD_DIM))
    v = v.reshape(B, T, GDN_HEADS, GDN_HEAD_DIM)
    beta = jax.nn.sigmoid(b.astype(F32))
    g = -jnp.exp(a_log_l.astype(F32)) * jax.nn.softplus(a.astype(F32) + dt_bias_l.astype(F32))
    return q, k, v, g, beta


def _gdn_chunked(q, k, v, g, beta):
    B, S, H, dk = q.shape
    dv = v.shape[-1]
    C = GDN_CHUNK
    n = S // C

    def chunks(t):
        t = jnp.moveaxis(t, 2, 1)
        return t.reshape((B, H, n, C) + t.shape[3:])

    qc, kc, vc, bc = chunks(q), chunks(k), chunks(v), chunks(beta)
    gc = jnp.cumsum(chunks(g), axis=-1)
    causal = jnp.tril(jnp.ones((C, C), bool))
    strict = jnp.tril(jnp.ones((C, C), bool), -1)
    diff = gc[..., :, None] - gc[..., None, :]
    decay = jnp.where(causal, jnp.exp(jnp.where(causal, diff, 0.0)), 0.0)
    kb = kc * bc[..., None]
    lower = jnp.where(strict, jnp.einsum('bhnid,bhnjd->bhnij', kb, kc) * decay, 0.0)
    t_mat = lower + jnp.eye(C, dtype=F32)
    u = lax.linalg.triangular_solve(t_mat, vc * bc[..., None], left_side=True, lower=True, unit_diagonal=True)
    wk = lax.linalg.triangular_solve(t_mat, kb * jnp.exp(gc)[..., None], left_side=True, lower=True, unit_diagonal=True)
    intra = jnp.where(causal, jnp.einsum('bhnid,bhnjd->bhnij', qc, kc) * decay, 0.0)

    def step(state, inp):
        q_i, k_i, u_i, w_i, g_i, a_i = inp
        v_new = u_i - jnp.einsum('bhcd,bhde->bhce', w_i, state)
        o = (jnp.einsum('bhcd,bhde->bhce', q_i * jnp.exp(g_i)[..., None], state)
             + jnp.einsum('bhij,bhje->bhie', a_i, v_new))
        g_last = g_i[..., -1]
        state = (state * jnp.exp(g_last)[..., None, None]
                 + jnp.einsum('bhcd,bhce->bhde', k_i * jnp.exp(g_last[..., None] - g_i)[..., None], v_new))
        return state, o

    xs = tuple(jnp.moveaxis(t, 2, 0) for t in (qc, kc, u, wk, gc, intra))
    s_fin, o = lax.scan(step, jnp.zeros((B, H, dk, dv), F32), xs)
    o = jnp.moveaxis(o, 0, 2).reshape(B, H, S, dv)
    return jnp.moveaxis(o, 1, 2), s_fin


def _gdn_recurrent(q, k, v, g, beta, state):
    xs = tuple(jnp.moveaxis(t, 1, 0) for t in (q, k, v, g, beta))

    def step(S, inp):
        q_t, k_t, v_t, g_t, b_t = inp
        S = S * jnp.exp(g_t)[..., None, None]
        delta = (v_t - jnp.einsum('bhd,bhde->bhe', k_t, S)) * b_t[..., None]
        S = S + jnp.einsum('bhd,bhe->bhde', k_t, delta)
        return S, jnp.einsum('bhd,bhde->bhe', q_t, S)

    S, o = lax.scan(step, state.astype(F32), xs)
    return jnp.moveaxis(o, 0, 1), S


def _mixer_output(x, gate, o_sb, sb_gate, o_gdn, gdn_z, gdn_norm_w_l, w_out_l, ln_g_l, ln_b_l):
    B, T, _ = x.shape
    y_sb = o_sb.reshape(B, T, SB_WIDTH) * jax.nn.silu(sb_gate.astype(F32))
    y_gdn = _rmsnorm(o_gdn, gdn_norm_w_l).reshape(B, T, GDN_WIDTH) * jax.nn.silu(gdn_z.astype(F32))
    mixed = jnp.concatenate([y_sb, y_gdn], axis=-1).astype(x.dtype) @ w_out_l
    return _layernorm(DN_ALPHA * x + gate * mixed, ln_g_l, ln_b_l)


def setup_inputs(seed: int = 0) -> dict:
    key = jax.random.key(seed)
    ks = jax.random.split(key, 21)
    n_pages = PAST_LEN // PAGE_SIZE
    n_used = DEC_BATCH * n_pages
    n_pool = n_used + max(1, n_used // 4)
    nrm = jax.random.normal
    x_prompt = nrm(ks[0], (BATCH, SEQ, D_MODEL), F32)
    x_sample = nrm(ks[1], (DEC_BATCH, DEC_SEQ, D_MODEL), F32)
    cache_k = nrm(ks[2], (DEPTH, n_pool, PAGE_SIZE, SB_HEADS, SB_HEAD_DIM), F32)
    cache_v = nrm(ks[3], (DEPTH, n_pool, PAGE_SIZE, SB_HEADS, SB_HEAD_DIM), F32)
    page_table = jax.random.permutation(ks[4], n_pool)[:n_used].reshape(DEC_BATCH, n_pages).astype(jnp.int32)
    state_conv = nrm(ks[5], (DEPTH, DEC_BATCH, CONV_W - 1, GDN_CONV_DIM), F32)
    state_rec = 0.1 * nrm(ks[6], (DEPTH, DEC_BATCH, GDN_HEADS, GDN_HEAD_DIM, GDN_HEAD_DIM), F32)
    c_prompt = nrm(ks[7], (BATCH, D_MODEL), F32)
    c_sample = nrm(ks[8], (DEC_BATCH, D_MODEL), F32)
    w_mod = 0.5 * D_MODEL ** -0.5 * nrm(ks[9], (DEPTH, D_MODEL, 3 * D_MODEL), F32)
    b_mod = 0.01 * nrm(ks[10], (DEPTH, 3 * D_MODEL), F32)
    w_in = D_MODEL ** -0.5 * nrm(ks[11], (DEPTH, D_MODEL, D_IN), F32)
    sb_bias = jax.random.uniform(ks[19], (DEPTH, SB_HEADS), F32, SB_BIAS_LO, SB_BIAS_HI)
    conv_w = CONV_W ** -0.5 * nrm(ks[12], (DEPTH, CONV_W, GDN_CONV_DIM), F32)
    a_log = jnp.log(jax.random.uniform(ks[13], (DEPTH, GDN_HEADS), F32, 1.0, 16.0))
    dt = jnp.exp(jax.random.uniform(ks[14], (DEPTH, GDN_HEADS), F32, np.log(1e-3), np.log(1e-1)))
    dt_bias = dt + jnp.log(-jnp.expm1(-dt))
    gdn_norm_w = 1.0 + 0.02 * nrm(ks[15], (DEPTH, GDN_HEAD_DIM), F32)
    w_out = DN_BETA * D_MIX ** -0.5 * nrm(ks[16], (DEPTH, D_MIX, D_MODEL), F32)
    ln_g = 1.0 + 0.02 * nrm(ks[17], (DEPTH, D_MODEL), F32)
    ln_b = 0.02 * nrm(ks[18], (DEPTH, D_MODEL), F32)
    return {"x_prompt": x_prompt, "x_sample": x_sample, "cache_k": cache_k, "cache_v": cache_v,
            "page_table": page_table, "state_conv": state_conv, "state_rec": state_rec,
            "c_prompt": c_prompt, "c_sample": c_sample, "w_mod": w_mod, "b_mod": b_mod, "w_in": w_in,
            "sb_bias": sb_bias, "conv_w": conv_w, "a_log": a_log, "dt_bias": dt_bias,
            "gdn_norm_w": gdn_norm_w, "w_out": w_out, "ln_g": ln_g, "ln_b": ln_b}


def reference(x_prompt, x_sample, cache_k, cache_v, page_table, state_conv, state_rec, c_prompt, c_sample,
              w_mod, b_mod, w_in, sb_bias, conv_w, a_log, dt_bias, gdn_norm_w, w_out, ln_g, ln_b):
    B, S, _ = x_prompt.shape
    DB, T, _ = x_sample.shape
    past = page_table.shape[1] * cache_k.shape[2]
    hp, hs = x_prompt, x_sample
    kp_l, vp_l, cp_l, rp_l, ks_l, vs_l, cs_l, rs_l = [], [], [], [], [], [], [], []
    for l in range(DEPTH):
        gate, (sq, sk, sv, sg, qkv, gz, gb, ga) = _mixer_inputs(hp, c_prompt, w_mod[l], b_mod[l], w_in[l])
        sq = sq.reshape(B, S, SB_HEADS, SB_HEAD_DIM)
        sk = sk.reshape(B, S, SB_HEADS, SB_HEAD_DIM)
        sv = sv.reshape(B, S, SB_HEADS, SB_HEAD_DIM)
        o_sb = _sb_prompt(sq, sk, sv, sb_bias[l])
        qkv_c, conv_p = _causal_conv(qkv, jnp.zeros((B, CONV_W - 1, GDN_CONV_DIM), qkv.dtype), conv_w[l])
        gq, gk, gv, gg, gbeta = _gdn_inputs(qkv_c, ga, gb, a_log[l], dt_bias[l])
        o_gdn, rec_p = _gdn_chunked(gq, gk, gv, gg, gbeta)
        hp = _mixer_output(hp, gate, o_sb, sg, o_gdn, gz, gdn_norm_w[l], w_out[l], ln_g[l], ln_b[l])
        kp_l.append(sk)
        vp_l.append(sv)
        cp_l.append(conv_p.astype(state_conv.dtype))
        rp_l.append(rec_p.astype(state_rec.dtype))
        gate, (sq, sk, sv, sg, qkv, gz, gb, ga) = _mixer_inputs(hs, c_sample, w_mod[l], b_mod[l], w_in[l])
        sq = sq.reshape(DB, T, SB_HEADS, SB_HEAD_DIM)
        sk = sk.reshape(DB, T, SB_HEADS, SB_HEAD_DIM)
        sv = sv.reshape(DB, T, SB_HEADS, SB_HEAD_DIM)
        k_past = cache_k[l][page_table].reshape(DB, past, SB_HEADS, SB_HEAD_DIM)
        v_past = cache_v[l][page_table].reshape(DB, past, SB_HEADS, SB_HEAD_DIM)
        o_sb = _sb_sample(sq, sk, sv, k_past, v_past, sb_bias[l])
        qkv_c, conv_s = _causal_conv(qkv, state_conv[l], conv_w[l])
        gq, gk, gv, gg, gbeta = _gdn_inputs(qkv_c, ga, gb, a_log[l], dt_bias[l])
        o_gdn, rec_s = _gdn_recurrent(gq, gk, gv, gg, gbeta, state_rec[l])
        hs = _mixer_output(hs, gate, o_sb, sg, o_gdn, gz, gdn_norm_w[l], w_out[l], ln_g[l], ln_b[l])
        ks_l.append(sk.astype(cache_k.dtype))
        vs_l.append(sv.astype(cache_v.dtype))
        cs_l.append(conv_s.astype(state_conv.dtype))
        rs_l.append(rec_s.astype(state_rec.dtype))
    return (hp, hs, jnp.stack(kp_l), jnp.stack(vp_l), jnp.stack(cp_l), jnp.stack(rp_l),
            jnp.stack(ks_l), jnp.stack(vs_l), jnp.stack(cs_l), jnp.stack(rs_l))
```

```python
import functools

import jax
import jax.numpy as jnp
from jax import lax
from jax.experimental import pallas as pl
from jax.experimental.pallas import tpu as pltpu

F32 = jnp.float32
BF16 = jnp.bfloat16

D_MODEL = 1024
DEPTH = 2
SB_HEADS = 8
SB_HEAD_DIM = 64
SB_WIDTH = SB_HEADS * SB_HEAD_DIM
SB_SCALE = SB_HEAD_DIM ** -0.5
GDN_HEADS = 4
GDN_HEAD_DIM = 128
GDN_WIDTH = GDN_HEADS * GDN_HEAD_DIM
GDN_CONV_DIM = 3 * GDN_WIDTH
CONV_W = 4
GDN_CHUNK = 64
DN_ALPHA = (2 * DEPTH) ** 0.25
LN_EPS = 1e-5
RMS_EPS = 1e-6
L2_EPS = 1e-6

LANES = 128
SUBLANES = 8
COL_Q, COL_K, COL_V, COL_G = 0, SB_WIDTH, 2 * SB_WIDTH, 3 * SB_WIDTH
COL_QKV = 4 * SB_WIDTH
COL_Z = COL_QKV + GDN_CONV_DIM
COL_BA = COL_Z + GDN_WIDTH
W_IN_PAD = COL_BA + LANES
VMEM_LIMIT = 56 * 1024 * 1024

_NT = (((1,), (1,)), ((), ()))
_TN = (((0,), (0,)), ((), ()))


def _silu(x):
    return x * jax.nn.sigmoid(x)


def _softplus(x):
    return jnp.maximum(x, 0.0) + jnp.log1p(jnp.exp(-jnp.abs(x)))


def _split_bf16(x):
    hi = x.astype(BF16)
    lo = (x - hi.astype(F32)).astype(BF16)
    return hi, lo


def _params(sem):
    return pltpu.CompilerParams(dimension_semantics=sem, vmem_limit_bytes=VMEM_LIMIT)


def _mod_kernel(c_ref, w_ref, b_ref, o_ref):
    a = _silu(c_ref[...]).astype(BF16)
    o_ref[...] = jnp.dot(a, w_ref[...].astype(BF16), preferred_element_type=F32) + b_ref[...]


def _mod_call(c_all, w_mod, b_mod):
    n = c_all.shape[0]
    b4 = b_mod.reshape(DEPTH, 3, 1, D_MODEL)
    return pl.pallas_call(
        _mod_kernel,
        grid=(DEPTH, 3),
        in_specs=[
            pl.BlockSpec((n, D_MODEL), lambda l, p: (0, 0)),
            pl.BlockSpec((None, D_MODEL, D_MODEL), lambda l, p: (l, 0, p)),
            pl.BlockSpec((None, None, 1, D_MODEL), lambda l, p: (l, p, 0, 0)),
        ],
        out_specs=pl.BlockSpec((None, None, n, D_MODEL), lambda l, p: (l, p, 0, 0)),
        out_shape=jax.ShapeDtypeStruct((DEPTH, 3, n, D_MODEL), F32),
        compiler_params=_params(("arbitrary", "arbitrary")),
        name="adaln_mod",
    )(c_all, w_mod, b4)


SB_TQ = 128
SB_TK = 128


def _inproj_kernel(transposed_kv, x_ref, shift_ref, scale_ref, w_ref, wt_ref, *out_refs):
    u = (x_ref[...] * (1.0 + scale_ref[...]) + shift_ref[...]).astype(BF16)

    def seg(lo, width):
        return jnp.dot(u, w_ref[:, lo:lo + width], preferred_element_type=F32)

    if transposed_kv:
        q_ref, kt_ref, vt_ref, ktb_ref, vtb_ref, sg_ref, qkv_ref, gz_ref, gba_ref, gbat_ref = out_refs
        t = lax.dot_general(wt_ref[...], u, _NT, preferred_element_type=F32)
        kt = t[:SB_WIDTH]
        vt = t[SB_WIDTH:2 * SB_WIDTH]
        kt_ref[...] = kt
        vt_ref[...] = vt
        for j in range(u.shape[0] // SB_TK):
            ktb_ref[j] = kt[:, j * SB_TK:(j + 1) * SB_TK].astype(BF16)
            vtb_ref[j] = vt[:, j * SB_TK:(j + 1) * SB_TK].astype(BF16)
        gbat_ref[...] = t[2 * SB_WIDTH:]
    else:
        q_ref, k_ref, v_ref, sg_ref, qkv_ref, gz_ref, gba_ref = out_refs
        k_ref[...] = seg(COL_K, SB_WIDTH)
        v_ref[...] = seg(COL_V, SB_WIDTH)
    q_ref[...] = (seg(COL_Q, SB_WIDTH) * SB_SCALE).astype(BF16)
    sg_ref[...] = seg(COL_G, SB_WIDTH)
    for j in range(3):
        qkv_ref[:, j * GDN_WIDTH:(j + 1) * GDN_WIDTH] = seg(COL_QKV + j * GDN_WIDTH, GDN_WIDTH)
    gz_ref[...] = seg(COL_Z, GDN_WIDTH)
    gba_ref[...] = seg(COL_BA, LANES)


def _inproj_call(x, shift, scale, w_bf, wt_bf, tm, per_row_mod, transposed_kv):
    g, t, _ = x.shape
    nt = t // tm
    if per_row_mod:
        mod_spec = pl.BlockSpec((None, tm, D_MODEL), lambda b, i: (b, i, 0))
    else:
        mod_spec = pl.BlockSpec((None, 1, D_MODEL), lambda b, i: (b, 0, 0))

    def tok(width, dtype):
        return (pl.BlockSpec((None, tm, width), lambda b, i: (b, i, 0)),
                jax.ShapeDtypeStruct((g, t, width), dtype))

    def chan(rows):
        return (pl.BlockSpec((None, rows, tm), lambda b, i: (b, 0, i)),
                jax.ShapeDtypeStruct((g, rows, t), F32))

    def blocks():
        return (pl.BlockSpec((None, tm // SB_TK, SB_WIDTH, SB_TK), lambda b, i: (b, i, 0, 0)),
                jax.ShapeDtypeStruct((g, t // SB_TK, SB_WIDTH, SB_TK), BF16))

    tail = [tok(SB_WIDTH, F32), tok(GDN_CONV_DIM, F32), tok(GDN_WIDTH, F32), tok(LANES, F32)]
    if transposed_kv:
        outs = [tok(SB_WIDTH, BF16), chan(SB_WIDTH), chan(SB_WIDTH), blocks(), blocks()] + tail
        outs.append(chan(SUBLANES))
    else:
        outs = [tok(SB_WIDTH, BF16), tok(SB_WIDTH, F32), tok(SB_WIDTH, F32)] + tail
    return pl.pallas_call(
        functools.partial(_inproj_kernel, transposed_kv),
        grid=(g, nt),
        in_specs=[
            pl.BlockSpec((None, tm, D_MODEL), lambda b, i: (b, i, 0)),
            mod_spec, mod_spec,
            pl.BlockSpec((D_MODEL, W_IN_PAD), lambda b, i: (0, 0)),
            pl.BlockSpec(wt_bf.shape, lambda b, i: (0, 0)),
        ],
        out_specs=[o[0] for o in outs],
        out_shape=[o[1] for o in outs],
        compiler_params=_params(("arbitrary", "arbitrary")),
        name="in_proj",
    )(x, shift, scale, w_bf, wt_bf)


def _cumsum_matrix():
    j = lax.broadcasted_iota(jnp.int32, (2 * SB_TK, 2 * SB_TK), 0) % SB_TK
    s = lax.broadcasted_iota(jnp.int32, (2 * SB_TK, 2 * SB_TK), 1)
    return jnp.where((s >= SB_TK) | (j > s), 1.0, 0.0).astype(BF16)


def _sb_block(qh, kt_blk, vt_blk, bias, uu, carry, acc, allowed):
    z = jnp.dot(qh, kt_blk, preferred_element_type=F32) + bias
    sp = _softplus(z)
    rest = sp if allowed is None else jnp.where(allowed, sp, 0.0)
    hi, lo = _split_bf16(rest)
    cs = jnp.dot(jnp.concatenate([hi, lo], axis=1), uu, preferred_element_type=F32)
    tk = kt_blk.shape[1]
    e = jnp.exp(z - sp - carry - cs[:, :tk])
    w = e if allowed is None else jnp.where(allowed, e, 0.0)
    acc = acc + lax.dot_general(w.astype(BF16), vt_blk, _NT, preferred_element_type=F32)
    return carry + cs[:, tk:], acc


def _sb_prompt_kernel(bias_ref, q_ref, k_ref, v_ref, sg_ref, uu_ref, y_ref):
    pair = pl.program_id(1)
    qi = pl.program_id(2)
    q = q_ref[...]
    uu = uu_ref[...]
    lane = lax.broadcasted_iota(jnp.int32, (SB_TQ, LANES), 1)
    row = lax.broadcasted_iota(jnp.int32, (SB_TQ, SB_TK), 0)
    col = lax.broadcasted_iota(jnp.int32, (SB_TQ, SB_TK), 1)
    allowed = col < row
    zeros = jnp.zeros((SB_TQ, LANES), F32)
    heads = []
    for h in range(2):
        in_head = lane // SB_HEAD_DIM == h
        qh = q * jnp.where(in_head, 1.0, 0.0).astype(BF16)
        bias = bias_ref[2 * pair + h]

        def tile(kb, carry, acc, mask):
            return _sb_block(qh, k_ref[kb], v_ref[kb], bias, uu, carry, acc, mask)

        carry, acc = tile(qi, zeros, zeros, allowed)
        carry, acc = lax.fori_loop(
            0, qi, lambda i, c: tile(qi - 1 - i, c[0], c[1], None), (carry, acc))
        heads.append(acc)
    o = jnp.where(lane < SB_HEAD_DIM, heads[0], heads[1])
    y_ref[...] = (o * _silu(sg_ref[...])).astype(BF16)


def _sb_prompt_call(q_bf, kt_bf, vt_bf, sg, bias, uu):
    b, s, _ = q_bf.shape
    nq = s // SB_TQ
    npair = SB_WIDTH // LANES
    qspec = pl.BlockSpec((None, SB_TQ, LANES), lambda bi, p, i: (bi, i, p))
    kvspec = pl.BlockSpec((None, s // SB_TK, LANES, SB_TK), lambda bi, p, i: (bi, 0, p, 0))
    return pl.pallas_call(
        _sb_prompt_kernel,
        grid=(b, npair, nq),
        in_specs=[pl.BlockSpec(memory_space=pltpu.SMEM), qspec, kvspec, kvspec, qspec,
                  pl.BlockSpec((2 * SB_TK, 2 * SB_TK), lambda bi, p, i: (0, 0))],
        out_specs=qspec,
        out_shape=jax.ShapeDtypeStruct((b, s, SB_WIDTH), BF16),
        compiler_params=_params(("arbitrary", "arbitrary", "arbitrary")),
        name="sb_prompt",
    )(bias, q_bf, kt_bf, vt_bf, sg, uu)


GDN_TS = 256
GDN_NCH = GDN_TS // GDN_CHUNK


def _chunk_cumsum_matrix(ts):
    j = lax.broadcasted_iota(jnp.int32, (ts, ts), 0)
    s = lax.broadcasted_iota(jnp.int32, (ts, ts), 1)
    return jnp.where((j <= s) & (j // GDN_CHUNK == s // GDN_CHUNK), 1.0, 0.0).astype(BF16)


def _l2norm(x):
    return x * lax.rsqrt(jnp.sum(x * x, axis=-1, keepdims=True) + L2_EPS)


def _gdn_out(o, gz, norm_w):
    y = o * lax.rsqrt(jnp.mean(o * o, axis=-1, keepdims=True) + RMS_EPS) * norm_w
    return y * _silu(gz)


def _gdn_chunk(qq, kk, vv, bcol, gcol, grow, state, causal, strict):
    c = qq.shape[0]
    dk = qq.shape[1]
    diff = gcol - grow
    decay = jnp.where(causal, jnp.exp(jnp.where(causal, diff, 0.0)), 0.0)
    kb = kk * bcol
    kk_bf = kk.astype(BF16)
    akk = lax.dot_general(kb.astype(BF16), kk_bf, _NT, preferred_element_type=F32)
    x = -jnp.where(strict, akk * decay, 0.0)
    n = x
    p = x
    for _ in range(5):
        pb = p.astype(BF16)
        p = jnp.dot(pb, pb, preferred_element_type=F32)
        n = n + p + jnp.dot(n.astype(BF16), p.astype(BF16), preferred_element_type=F32)
    egc = jnp.exp(gcol)
    rhs = jnp.concatenate([vv * bcol, kb * egc], axis=1)
    uw = rhs + jnp.dot(n.astype(BF16), rhs.astype(BF16), preferred_element_type=F32)
    u = uw[:, :dk]
    wk = uw[:, dk:]
    intra = jnp.where(
        causal, lax.dot_general(qq.astype(BF16), kk_bf, _NT, preferred_element_type=F32) * decay, 0.0)
    glast = grow[:, c - 1:c]
    sb = state.astype(BF16)
    v_new = u - jnp.dot(wk.astype(BF16), sb, preferred_element_type=F32)
    vnb = v_new.astype(BF16)
    o = (jnp.dot((qq * egc).astype(BF16), sb, preferred_element_type=F32)
         + jnp.dot(intra.astype(BF16), vnb, preferred_element_type=F32))
    kgl = (kk * jnp.exp(glast - gcol)).astype(BF16)
    new_state = state * jnp.exp(glast) + lax.dot_general(kgl, vnb, _TN, preferred_element_type=F32)
    return o, new_state


def _gdn_prompt_kernel(qkv_ref, gz_ref, gba_ref, gbat_ref, cw_ref, hrow_ref, hcol_ref, nw_ref,
                       ub_ref, lb_ref, y_ref, conv_ref, rec_ref, xp_ref, st_ref):
    si = pl.program_id(1)
    ts = GDN_TS
    c = GDN_CHUNK
    pad = SUBLANES

    @pl.when(si == 0)
    def _():
        xp_ref[0:pad, :] = jnp.zeros((pad, GDN_CONV_DIM), F32)
        st_ref[...] = jnp.zeros_like(st_ref)

    xp_ref[pad:pad + ts, :] = qkv_ref[...]
    y = xp_ref[pad:pad + ts, :] * cw_ref[CONV_W - 1:CONV_W, :]
    for i in range(CONV_W - 1):
        sh = CONV_W - 1 - i
        y = y + xp_ref[pad - sh:pad - sh + ts, :] * cw_ref[i:i + 1, :]
    qkv = _silu(y)
    tail = xp_ref[ts:ts + pad, :]
    xp_ref[0:pad, :] = tail

    gba = gba_ref[...]
    beta_c = jax.nn.sigmoid(gba)
    g_c = -jnp.exp(hrow_ref[0:1, :]) * _softplus(gba + hrow_ref[1:2, :])
    gbat = gbat_ref[...]
    g_r = -jnp.exp(hcol_ref[:, 0:1]) * _softplus(gbat + hcol_ref[:, 1:2])
    ub = ub_ref[...]
    g_hi, g_lo = _split_bf16(g_c)
    lb = lb_ref[...]
    gc_c = (jnp.dot(lb, g_hi, preferred_element_type=F32)
            + jnp.dot(lb, g_lo, preferred_element_type=F32))
    r_hi, r_lo = _split_bf16(g_r)
    gc_r = (jnp.dot(r_hi, ub, preferred_element_type=F32)
            + jnp.dot(r_lo, ub, preferred_element_type=F32))

    ri = lax.broadcasted_iota(jnp.int32, (c, c), 0)
    ci = lax.broadcasted_iota(jnp.int32, (c, c), 1)
    causal = ci <= ri
    strict = ci < ri
    nw = nw_ref[...]
    gz = gz_ref[...]
    for h in range(GDN_HEADS):
        lo = h * GDN_HEAD_DIM
        hi = lo + GDN_HEAD_DIM
        q_h = _l2norm(qkv[:, lo:hi]) * GDN_HEAD_DIM ** -0.5
        k_h = _l2norm(qkv[:, GDN_WIDTH + lo:GDN_WIDTH + hi])
        v_h = qkv[:, 2 * GDN_WIDTH + lo:2 * GDN_WIDTH + hi]
        state = st_ref[h]
        for ch in range(GDN_NCH):
            r0 = ch * c
            o, state = _gdn_chunk(
                q_h[r0:r0 + c], k_h[r0:r0 + c], v_h[r0:r0 + c],
                beta_c[r0:r0 + c, h:h + 1],
                gc_c[r0:r0 + c, GDN_HEADS + h:GDN_HEADS + h + 1],
                gc_r[GDN_HEADS + h:GDN_HEADS + h + 1, r0:r0 + c],
                state, causal, strict)
            y_ref[r0:r0 + c, lo:hi] = _gdn_out(o, gz[r0:r0 + c, lo:hi], nw).astype(BF16)
        st_ref[h] = state

    @pl.when(si == pl.num_programs(1) - 1)
    def _():
        conv_ref[...] = tail[pad - (CONV_W - 1):pad, :]
        rec_ref[...] = st_ref[...]


def _gdn_prompt_call(qkv, gz, gba, gbat, cw, hrow, hcol, nw, ub, lb):
    b, s, _ = qkv.shape
    ns = s // GDN_TS
    const2 = lambda shape: pl.BlockSpec(shape, lambda bi, i: (0, 0))
    return pl.pallas_call(
        _gdn_prompt_kernel,
        grid=(b, ns),
        in_specs=[
            pl.BlockSpec((None, GDN_TS, GDN_CONV_DIM), lambda bi, i: (bi, i, 0)),
            pl.BlockSpec((None, GDN_TS, GDN_WIDTH), lambda bi, i: (bi, i, 0)),
            pl.BlockSpec((None, GDN_TS, LANES), lambda bi, i: (bi, i, 0)),
            pl.BlockSpec((None, SUBLANES, GDN_TS), lambda bi, i: (bi, 0, i)),
            const2((CONV_W, GDN_CONV_DIM)), const2((2, LANES)), const2((SUBLANES, 2)),
            const2((1, GDN_HEAD_DIM)), const2((GDN_TS, GDN_TS)), const2((GDN_TS, GDN_TS)),
        ],
        out_specs=[
            pl.BlockSpec((None, GDN_TS, GDN_WIDTH), lambda bi, i: (bi, i, 0)),
            pl.BlockSpec((None, CONV_W - 1, GDN_CONV_DIM), lambda bi, i: (bi, 0, 0)),
            pl.BlockSpec((None, GDN_HEADS, GDN_HEAD_DIM, GDN_HEAD_DIM), lambda bi, i: (bi, 0, 0, 0)),
        ],
        out_shape=[
            jax.ShapeDtypeStruct((b, s, GDN_WIDTH), BF16),
            jax.ShapeDtypeStruct((b, CONV_W - 1, GDN_CONV_DIM), F32),
            jax.ShapeDtypeStruct((b, GDN_HEADS, GDN_HEAD_DIM, GDN_HEAD_DIM), F32),
        ],
        scratch_shapes=[
            pltpu.VMEM((GDN_TS + SUBLANES, GDN_CONV_DIM), F32),
            pltpu.VMEM((GDN_HEADS, GDN_HEAD_DIM, GDN_HEAD_DIM), F32),
        ],
        compiler_params=_params(("arbitrary", "arbitrary")),
        name="gdn_prompt",
    )(qkv, gz, gba, gbat, cw, hrow, hcol, nw, ub, lb)


def _outproj_kernel(ysb_ref, ygdn_ref, x_ref, gate_ref, wt_ref, wb_ref, g_ref, b_ref, o_ref):
    mixed = (jnp.dot(ysb_ref[...], wt_ref[...], preferred_element_type=F32)
             + jnp.dot(ygdn_ref[...], wb_ref[...], preferred_element_type=F32))
    r = DN_ALPHA * x_ref[...] + gate_ref[...] * mixed
    mu = jnp.mean(r, axis=-1, keepdims=True)
    rc = r - mu
    var = jnp.mean(rc * rc, axis=-1, keepdims=True)
    o_ref[...] = rc * lax.rsqrt(var + LN_EPS) * g_ref[...] + b_ref[...]


def _outproj_call(y_sb, y_gdn, x, gate, w_top, w_bot, ln_g, ln_b, tm, per_row_mod):
    g, t, _ = x.shape
    nt = t // tm
    if per_row_mod:
        gate_spec = pl.BlockSpec((None, tm, D_MODEL), lambda b, i: (b, i, 0))
    else:
        gate_spec = pl.BlockSpec((None, 1, D_MODEL), lambda b, i: (b, 0, 0))
    half = pl.BlockSpec((None, tm, SB_WIDTH), lambda b, i: (b, i, 0))
    full = pl.BlockSpec((None, tm, D_MODEL), lambda b, i: (b, i, 0))
    const = lambda shape: pl.BlockSpec(shape, lambda b, i: (0, 0))
    return pl.pallas_call(
        _outproj_kernel,
        grid=(g, nt),
        in_specs=[half, half, full, gate_spec, const((SB_WIDTH, D_MODEL)), const((GDN_WIDTH, D_MODEL)),
                  const((1, D_MODEL)), const((1, D_MODEL))],
        out_specs=full,
        out_shape=jax.ShapeDtypeStruct((g, t, D_MODEL), F32),
        compiler_params=_params(("arbitrary", "arbitrary")),
        name="out_proj_ln",
    )(y_sb, y_gdn, x, gate, w_top, w_bot, ln_g, ln_b)


def _gdn_step_kernel(qkv_ref, cs_ref, gz_ref, gba_ref, rec_ref, cw_ref, hrow_ref, nw_ref,
                     y_ref, cs_out_ref, rec_out_ref):
    cdim = GDN_CONV_DIM
    x = qkv_ref[...]
    y = x * cw_ref[CONV_W - 1:CONV_W, :]
    for i in range(CONV_W - 1):
        y = y + cs_ref[:, i * cdim:(i + 1) * cdim] * cw_ref[i:i + 1, :]
    qkv = _silu(y)
    for i in range(CONV_W - 2):
        cs_out_ref[:, i * cdim:(i + 1) * cdim] = cs_ref[:, (i + 1) * cdim:(i + 2) * cdim]
    cs_out_ref[:, (CONV_W - 2) * cdim:(CONV_W - 1) * cdim] = x

    gba = gba_ref[...]
    beta = jax.nn.sigmoid(gba)
    g = -jnp.exp(hrow_ref[0:1, :]) * _softplus(gba + hrow_ref[1:2, :])
    eye = (lax.broadcasted_iota(jnp.int32, (GDN_HEAD_DIM, GDN_HEAD_DIM), 0)
           == lax.broadcasted_iota(jnp.int32, (GDN_HEAD_DIM, GDN_HEAD_DIM), 1))

    def column(row):
        return jnp.sum(jnp.where(eye, row, 0.0), axis=1, keepdims=True)

    nw = nw_ref[...]
    gz = gz_ref[...]
    for h in range(GDN_HEADS):
        lo = h * GDN_HEAD_DIM
        hi = lo + GDN_HEAD_DIM
        q_h = _l2norm(qkv[:, lo:hi]) * GDN_HEAD_DIM ** -0.5
        k_h = _l2norm(qkv[:, GDN_WIDTH + lo:GDN_WIDTH + hi])
        v_h = qkv[:, 2 * GDN_WIDTH + lo:2 * GDN_WIDTH + hi]
        s = rec_ref[h] * jnp.exp(g[:, GDN_HEADS + h:GDN_HEADS + h + 1])
        kcol = column(k_h)
        delta = (v_h - jnp.sum(s * kcol, axis=0, keepdims=True)) * beta[:, h:h + 1]
        s = s + kcol * delta
        rec_out_ref[h] = s
        o = jnp.sum(s * column(q_h), axis=0, keepdims=True)
        y_ref[:, lo:hi] = _gdn_out(o, gz[:, lo:hi], nw).astype(BF16)


def _gdn_step_call(layer, qkv, conv_state, gz, gba, state_rec, cw, hrow, nw):
    db = qkv.shape[0]
    row = lambda width: pl.BlockSpec((None, 1, width), lambda b: (b, 0, 0))
    const = lambda shape: pl.BlockSpec(shape, lambda b: (0, 0))
    rec_shape = (GDN_HEADS, GDN_HEAD_DIM, GDN_HEAD_DIM)
    cs_width = (CONV_W - 1) * GDN_CONV_DIM
    return pl.pallas_call(
        _gdn_step_kernel,
        grid=(db,),
        in_specs=[row(GDN_CONV_DIM), row(cs_width), row(GDN_WIDTH), row(LANES),
                  pl.BlockSpec((None, None) + rec_shape, lambda b: (layer, b, 0, 0, 0)),
                  const((CONV_W, GDN_CONV_DIM)), const((2, LANES)), const((1, GDN_HEAD_DIM))],
        out_specs=[row(GDN_WIDTH), row(cs_width),
                   pl.BlockSpec((None,) + rec_shape, lambda b: (b, 0, 0, 0))],
        out_shape=[jax.ShapeDtypeStruct((db, 1, GDN_WIDTH), BF16),
                   jax.ShapeDtypeStruct((db, 1, cs_width), F32),
                   jax.ShapeDtypeStruct((db,) + rec_shape, F32)],
        compiler_params=_params(("arbitrary",)),
        name="gdn_step",
    )(qkv, conv_state, gz, gba, state_rec, cw, hrow, nw)


PAGES_PER_STEP = 8


def _sb_paged_kernel(pt_ref, q_ref, kn_ref, vn_ref, sg_ref, bias_ref, uu_ref, *rest):
    del pt_ref
    np_ = PAGES_PER_STEP
    k_refs = rest[:np_]
    v_refs = rest[np_:2 * np_]
    y_ref, qm_ref, carry_ref, acc_ref = rest[2 * np_:]
    gi = pl.program_id(1)
    head = lax.broadcasted_iota(jnp.int32, (SB_HEADS, SB_WIDTH), 0)
    chan = lax.broadcasted_iota(jnp.int32, (SB_HEADS, SB_WIDTH), 1)
    own = chan // SB_HEAD_DIM == head

    @pl.when(gi == 0)
    def _():
        qm_ref[...] = jnp.where(own, q_ref[...].astype(F32), 0.0)
        carry_ref[...] = jnp.zeros_like(carry_ref)
        acc_ref[...] = jnp.zeros_like(acc_ref)

    qm32 = qm_ref[...]
    qm = qm32.astype(BF16)
    bias = bias_ref[...]
    uu = uu_ref[...]
    carry = carry_ref[...]
    acc = acc_ref[...]
    for i in reversed(range(np_)):
        carry, acc = _sb_block(qm, k_refs[i][...].astype(BF16), v_refs[i][...].astype(BF16),
                               bias, uu, carry, acc, None)
    carry_ref[...] = carry
    acc_ref[...] = acc

    @pl.when(gi == pl.num_programs(1) - 1)
    def _():
        t_new = kn_ref.shape[0]
        z_new = jnp.sum(qm32 * kn_ref[...], axis=-1, keepdims=True) + bias
        q_idx = lax.broadcasted_iota(jnp.int32, (SB_HEADS, t_new), 1) + (t_new - 1)
        k_idx = lax.broadcasted_iota(jnp.int32, (SB_HEADS, t_new), 1)
        w_new = jnp.where(k_idx < q_idx, jnp.exp(z_new - _softplus(z_new)), 0.0)
        total = acc + w_new * vn_ref[...]
        o = jnp.sum(jnp.where(own, total, 0.0), axis=0, keepdims=True)
        y_ref[...] = (o * _silu(sg_ref[...])).astype(BF16)


def _sb_paged_call(layer, page_table, q_bf, k_new, v_new, sg, bias_col, uu, cache_k, cache_v):
    db, n_pages = page_table.shape
    np_ = PAGES_PER_STEP
    ng = n_pages // np_
    page = cache_k.shape[2]
    ck = jnp.transpose(cache_k, (0, 1, 3, 4, 2)).reshape(cache_k.shape[0], cache_k.shape[1], SB_WIDTH, page)
    cv = jnp.transpose(cache_v, (0, 1, 3, 4, 2)).reshape(cache_v.shape[0], cache_v.shape[1], SB_WIDTH, page)
    row = pl.BlockSpec((None, 1, SB_WIDTH), lambda b, g, pt: (b, 0, 0))

    def page_spec(i):
        return pl.BlockSpec(
            (None, None, SB_WIDTH, page),
            lambda b, g, pt: (layer, pt[b, (ng - 1 - g) * np_ + i], 0, 0))

    grid_spec = pltpu.PrefetchScalarGridSpec(
        num_scalar_prefetch=1,
        grid=(db, ng),
        in_specs=[row, row, row, row,
                  pl.BlockSpec((SB_HEADS, 1), lambda b, g, pt: (0, 0)),
                  pl.BlockSpec((2 * SB_TK, 2 * SB_TK), lambda b, g, pt: (0, 0))]
                 + [page_spec(i) for i in range(np_)] * 2,
        out_specs=row,
        scratch_shapes=[pltpu.VMEM((SB_HEADS, SB_WIDTH), F32),
                        pltpu.VMEM((SB_HEADS, SB_TK), F32),
                        pltpu.VMEM((SB_HEADS, SB_WIDTH), F32)],
    )
    return pl.pallas_call(
        _sb_paged_kernel,
        grid_spec=grid_spec,
        out_shape=jax.ShapeDtypeStruct((db, 1, SB_WIDTH), BF16),
        compiler_params=_params(("arbitrary", "arbitrary")),
        name="sb_paged",
    )(page_table, q_bf, k_new, v_new, sg, bias_col, uu, *([ck] * np_), *([cv] * np_))


def _pick_tile(t, cap):
    tm = min(t, cap)
    assert t % tm == 0
    return tm


def kernel(x_prompt, x_sample, cache_k, cache_v, page_table, state_conv, state_rec, c_prompt, c_sample,
           w_mod, b_mod, w_in, sb_bias, conv_w, a_log, dt_bias, gdn_norm_w, w_out, ln_g, ln_b):
    b, s, _ = x_prompt.shape
    db, t_new, _ = x_sample.shape
    assert t_new == 1 and s % GDN_TS == 0 and s % SB_TQ == 0
    assert page_table.shape[1] % PAGES_PER_STEP == 0 and cache_k.shape[2] == SB_TK

    mod = _mod_call(jnp.concatenate([c_sample, c_prompt], axis=0), w_mod, b_mod)
    mod_rows = mod.reshape(DEPTH, 3, db + b, 1, D_MODEL)

    uu = _cumsum_matrix()
    ub = _chunk_cumsum_matrix(GDN_TS)
    lb = ub.T
    gate_pad = ((0, 0), (GDN_HEADS, LANES - 2 * GDN_HEADS))

    hp, hs = x_prompt, x_sample.reshape(1, db, D_MODEL)
    kp_l, vp_l, cp_l, rp_l, ks_l, vs_l, cs_l, rs_l = [], [], [], [], [], [], [], []
    for l in range(DEPTH):
        w_bf = jnp.pad(w_in[l], ((0, 0), (0, W_IN_PAD - w_in.shape[2]))).astype(BF16)
        wt_bf = jnp.concatenate(
            [w_in[l][:, COL_K:COL_G], w_in[l][:, COL_BA:COL_BA + SUBLANES]], axis=1).T.astype(BF16)
        w_top = w_out[l][:SB_WIDTH].astype(BF16)
        w_bot = w_out[l][SB_WIDTH:].astype(BF16)
        hrow = jnp.pad(jnp.stack([a_log[l], dt_bias[l]]), gate_pad)
        hcol = hrow[:, :SUBLANES].T
        nw = gdn_norm_w[l].reshape(1, GDN_HEAD_DIM)
        lng = ln_g[l].reshape(1, D_MODEL)
        lnb = ln_b[l].reshape(1, D_MODEL)

        shift, scale, gate = (mod_rows[l, i, db:] for i in range(3))
        tm = _pick_tile(s, 256)
        q_bf, kt, vt, kt_bf, vt_bf, sg, qkv, gz, gba, gbat = _inproj_call(
            hp, shift, scale, w_bf, wt_bf, tm, per_row_mod=False, transposed_kv=True)
        y_sb = _sb_prompt_call(q_bf, kt_bf, vt_bf, sg, sb_bias[l], uu)
        y_gdn, conv_p, rec_p = _gdn_prompt_call(qkv, gz, gba, gbat, conv_w[l], hrow, hcol, nw, ub, lb)
        hp = _outproj_call(y_sb, y_gdn, hp, gate, w_top, w_bot, lng, lnb, _pick_tile(s, 512), False)
        heads_last = lambda a: jnp.transpose(a.reshape(b, SB_HEADS, SB_HEAD_DIM, s), (0, 3, 1, 2))
        kp_l.append(heads_last(kt))
        vp_l.append(heads_last(vt))
        cp_l.append(conv_p)
        rp_l.append(rec_p)

        shift, scale, gate = (mod[l, i, :db].reshape(1, db, D_MODEL) for i in range(3))
        q_bf, k, v, sg, qkv, gz, gba = _inproj_call(
            hs, shift, scale, w_bf, wt_bf, db, per_row_mod=True, transposed_kv=False)
        as_rows = lambda a: a.reshape(db, 1, a.shape[-1])
        y_sb = _sb_paged_call(l, page_table, as_rows(q_bf), as_rows(k), as_rows(v), as_rows(sg),
                              sb_bias[l].reshape(SB_HEADS, 1), uu, cache_k, cache_v)
        y_gdn, conv_s, rec_s = _gdn_step_call(
            l, as_rows(qkv), state_conv[l].reshape(db, 1, (CONV_W - 1) * GDN_CONV_DIM),
            as_rows(gz), as_rows(gba), state_rec, conv_w[l], hrow, nw)
        hs = _outproj_call(y_sb.reshape(1, db, SB_WIDTH), y_gdn.reshape(1, db, GDN_WIDTH), hs, gate,
                           w_top, w_bot, lng, lnb, db, True)
        ks_l.append(k.reshape(db, 1, SB_HEADS, SB_HEAD_DIM))
        vs_l.append(v.reshape(db, 1, SB_HEADS, SB_HEAD_DIM))
        cs_l.append(conv_s.reshape(db, CONV_W - 1, GDN_CONV_DIM))
        rs_l.append(rec_s)

    return (hp, hs.reshape(db, 1, D_MODEL), jnp.stack(kp_l), jnp.stack(vp_l), jnp.stack(cp_l),
            jnp.stack(rp_l), jnp.stack(ks_l), jnp.stack(vs_l), jnp.stack(cs_l), jnp.stack(rs_l))
```

```python
import functools

import jax
import jax.numpy as jnp
from jax import lax
from jax.experimental import pallas as pl
from jax.experimental.pallas import tpu as pltpu

F32 = jnp.float32
BF16 = jnp.bfloat16

D_MODEL = 1024
DEPTH = 2
SB_HEADS = 8
SB_HEAD_DIM = 64
SB_WIDTH = SB_HEADS * SB_HEAD_DIM
SB_SCALE = SB_HEAD_DIM ** -0.5
GDN_HEADS = 4
GDN_HEAD_DIM = 128
GDN_WIDTH = GDN_HEADS * GDN_HEAD_DIM
GDN_CONV_DIM = 3 * GDN_WIDTH
CONV_W = 4
DN_ALPHA = (2 * DEPTH) ** 0.25
LN_EPS = 1e-5
RMS_EPS = 1e-6
L2_EPS = 1e-6

LANES = 128
SUBLANES = 8
COL_Q, COL_K, COL_V, COL_G = 0, SB_WIDTH, 2 * SB_WIDTH, 3 * SB_WIDTH
COL_QKV = 4 * SB_WIDTH
COL_Z = COL_QKV + GDN_CONV_DIM
COL_BA = COL_Z + GDN_WIDTH
W_IN_PAD = COL_BA + LANES
VMEM_LIMIT = 56 * 1024 * 1024

_NT = (((1,), (1,)), ((), ()))
_TN = (((0,), (0,)), ((), ()))


def _silu(x):
    return x * jax.nn.sigmoid(x)


def _softplus(x):
    return jnp.maximum(x, 0.0) + jnp.log1p(jnp.exp(-jnp.abs(x)))


def _split_bf16(x):
    hi = x.astype(BF16)
    lo = (x - hi.astype(F32)).astype(BF16)
    return hi, lo


def _params(sem):
    return pltpu.CompilerParams(dimension_semantics=sem, vmem_limit_bytes=VMEM_LIMIT)


def _mod_kernel(c_ref, w_ref, b_ref, o_ref):
    a = _silu(c_ref[...]).astype(BF16)
    o_ref[...] = jnp.dot(a, w_ref[...].astype(BF16), preferred_element_type=F32) + b_ref[...]


def _mod_call(c_all, w_mod, b_mod):
    n = c_all.shape[0]
    b4 = b_mod.reshape(DEPTH, 3, 1, D_MODEL)
    return pl.pallas_call(
        _mod_kernel,
        grid=(DEPTH, 3),
        in_specs=[
            pl.BlockSpec((n, D_MODEL), lambda l, p: (0, 0)),
            pl.BlockSpec((None, D_MODEL, D_MODEL), lambda l, p: (l, 0, p)),
            pl.BlockSpec((None, None, 1, D_MODEL), lambda l, p: (l, p, 0, 0)),
        ],
        out_specs=pl.BlockSpec((None, None, n, D_MODEL), lambda l, p: (l, p, 0, 0)),
        out_shape=jax.ShapeDtypeStruct((DEPTH, 3, n, D_MODEL), F32),
        compiler_params=_params(("arbitrary", "arbitrary")),
        name="adaln_mod",
    )(c_all, w_mod, b4)


SB_TQ = 1024
SB_UNROLL = 2
SB_TK = 128
LOG2E = 1.4426950408889634


def _inproj_kernel(transposed_kv, x_ref, shift_ref, scale_ref, w_ref, wt_ref, *out_refs):
    u = (x_ref[...] * (1.0 + scale_ref[...]) + shift_ref[...]).astype(BF16)

    def seg(lo, width):
        return jnp.dot(u, w_ref[:, lo:lo + width], preferred_element_type=F32)

    if transposed_kv:
        q_ref, kt_ref, vt_ref, ktb_ref, vtb_ref, sg_ref, qkv_ref, gz_ref, gba_ref, gbat_ref = out_refs
        t = lax.dot_general(wt_ref[...], u, _NT, preferred_element_type=F32)
        kt = t[:SB_WIDTH]
        vt = t[SB_WIDTH:2 * SB_WIDTH]
        kt_ref[...] = kt
        vt_ref[...] = vt
        for j in range(u.shape[0] // SB_TK):
            ktb_ref[j] = kt[:, j * SB_TK:(j + 1) * SB_TK].astype(BF16)
            vtb_ref[j] = vt[:, j * SB_TK:(j + 1) * SB_TK].astype(BF16)
        gbat_ref[...] = t[2 * SB_WIDTH:]
    else:
        q_ref, k_ref, v_ref, sg_ref, qkv_ref, gz_ref, gba_ref = out_refs
        k_ref[...] = seg(COL_K, SB_WIDTH)
        v_ref[...] = seg(COL_V, SB_WIDTH)
    q_ref[...] = (seg(COL_Q, SB_WIDTH) * (SB_SCALE * LOG2E)).astype(BF16)
    sg_ref[...] = seg(COL_G, SB_WIDTH)
    for j in range(3):
        qkv_ref[:, j * GDN_WIDTH:(j + 1) * GDN_WIDTH] = seg(COL_QKV + j * GDN_WIDTH, GDN_WIDTH)
    gz_ref[...] = seg(COL_Z, GDN_WIDTH)
    gba_ref[...] = seg(COL_BA, LANES)


def _inproj_call(x, shift, scale, w_bf, wt_bf, tm, per_row_mod, transposed_kv):
    g, t, _ = x.shape
    nt = t // tm
    if per_row_mod:
        mod_spec = pl.BlockSpec((None, tm, D_MODEL), lambda b, i: (b, i, 0))
    else:
        mod_spec = pl.BlockSpec((None, 1, D_MODEL), lambda b, i: (b, 0, 0))

    def tok(width, dtype):
        return (pl.BlockSpec((None, tm, width), lambda b, i: (b, i, 0)),
                jax.ShapeDtypeStruct((g, t, width), dtype))

    def chan(rows):
        return (pl.BlockSpec((None, rows, tm), lambda b, i: (b, 0, i)),
                jax.ShapeDtypeStruct((g, rows, t), F32))

    def blocks():
        return (pl.BlockSpec((None, tm // SB_TK, SB_WIDTH, SB_TK), lambda b, i: (b, i, 0, 0)),
                jax.ShapeDtypeStruct((g, t // SB_TK, SB_WIDTH, SB_TK), BF16))

    tail = [tok(SB_WIDTH, F32), tok(GDN_CONV_DIM, F32), tok(GDN_WIDTH, F32), tok(LANES, F32)]
    if transposed_kv:
        outs = [tok(SB_WIDTH, BF16), chan(SB_WIDTH), chan(SB_WIDTH), blocks(), blocks()] + tail
        outs.append(chan(SUBLANES))
    else:
        outs = [tok(SB_WIDTH, BF16), tok(SB_WIDTH, F32), tok(SB_WIDTH, F32)] + tail
    return pl.pallas_call(
        functools.partial(_inproj_kernel, transposed_kv),
        grid=(g, nt),
        in_specs=[
            pl.BlockSpec((None, tm, D_MODEL), lambda b, i: (b, i, 0)),
            mod_spec, mod_spec,
            pl.BlockSpec((D_MODEL, W_IN_PAD), lambda b, i: (0, 0)),
            pl.BlockSpec(wt_bf.shape, lambda b, i: (0, 0)),
        ],
        out_specs=[o[0] for o in outs],
        out_shape=[o[1] for o in outs],
        compiler_params=_params(("arbitrary", "arbitrary")),
        name="in_proj",
    )(x, shift, scale, w_bf, wt_bf)


def _cumsum_matrix():
    j = lax.broadcasted_iota(jnp.int32, (2 * SB_TK, 2 * SB_TK), 0) % SB_TK
    s = lax.broadcasted_iota(jnp.int32, (2 * SB_TK, 2 * SB_TK), 1)
    return jnp.where((s >= SB_TK) | (j >= s), 1.0, 0.0).astype(BF16)


def _neg_abs(x):
    bits = lax.bitcast_convert_type(x, jnp.uint32) | jnp.uint32(0x80000000)
    return lax.bitcast_convert_type(bits, F32)


def _sb_scores(qh, kt_blk, bias2, uu, allowed):
    z = jnp.dot(qh, kt_blk, preferred_element_type=F32) + bias2
    sp = jnp.maximum(z, 0.0) + jnp.log(1.0 + jnp.exp2(_neg_abs(z))) * LOG2E
    rest = sp if allowed is None else jnp.where(allowed, sp, 0.0)
    hi, lo = _split_bf16(rest)
    return z, jnp.dot(jnp.concatenate([hi, lo], axis=1), uu, preferred_element_type=F32)


def _sb_apply(z, cs, carry, vt_blk, allowed):
    tk = vt_blk.shape[1]
    e = jnp.exp2(z - carry - cs[:, :tk])
    w = e if allowed is None else jnp.where(allowed, e, 0.0)
    pv = lax.dot_general(w.astype(BF16), vt_blk, _NT, preferred_element_type=F32)
    return carry + cs[:, tk:], pv


def _sb_prompt_kernel(bias_ref, q_ref, k_ref, v_ref, sg_ref, uu_ref, y_ref, qm_ref, carry_ref, acc_ref):
    pair = pl.program_id(1)
    qi = pl.program_id(2)
    nsub = SB_TQ // SB_TK
    q = q_ref[...]
    uu = uu_ref[...]
    lane = lax.broadcasted_iota(jnp.int32, (SB_TQ, LANES), 1)
    for h in range(2):
        qm_ref[h] = q * jnp.where(lane // SB_HEAD_DIM == h, 1.0, 0.0).astype(BF16)
    carry_ref[...] = jnp.zeros_like(carry_ref)
    acc_ref[...] = jnp.zeros_like(acc_ref)
    bias2 = [bias_ref[2 * pair + h] * LOG2E for h in range(2)]

    def sweep(kbs, first_row, on_diagonal):
        rows = slice(first_row, SB_TQ)
        allowed = None
        if on_diagonal:
            row = lax.broadcasted_iota(jnp.int32, (SB_TQ - first_row, SB_TK), 0)
            col = lax.broadcasted_iota(jnp.int32, (SB_TQ - first_row, SB_TK), 1)
            allowed = col < row
        scores = [[_sb_scores(qm_ref[h, rows, :], k_ref[kb], bias2[h], uu, allowed) for h in range(2)]
                  for kb in kbs]
        for kb, per_head in zip(kbs, scores):
            vt = v_ref[kb]
            for h, (z, cs) in enumerate(per_head):
                carry, pv = _sb_apply(z, cs, carry_ref[h, rows, :], vt, allowed)
                carry_ref[h, rows, :] = carry
                acc_ref[h, rows, :] += pv

    for j in reversed(range(nsub)):
        sweep([qi * nsub + j], j * SB_TK, True)

    def body(i, _):
        kb = qi * nsub - 1 - SB_UNROLL * i
        sweep([kb - u for u in range(SB_UNROLL)], 0, False)
        return 0

    lax.fori_loop(0, qi * (nsub // SB_UNROLL), body, 0)
    o = jnp.where(lane < SB_HEAD_DIM, acc_ref[0], acc_ref[1])
    y_ref[...] = (o * _silu(sg_ref[...])).astype(BF16)


def _sb_prompt_call(q_bf, kt_bf, vt_bf, sg, bias, uu):
    b, s, _ = q_bf.shape
    nq = s // SB_TQ
    npair = SB_WIDTH // LANES
    qspec = pl.BlockSpec((None, SB_TQ, LANES), lambda bi, p, i: (bi, i, p))
    kvspec = pl.BlockSpec((None, s // SB_TK, LANES, SB_TK), lambda bi, p, i: (bi, 0, p, 0))
    return pl.pallas_call(
        _sb_prompt_kernel,
        grid=(b, npair, nq),
        in_specs=[pl.BlockSpec(memory_space=pltpu.SMEM), qspec, kvspec, kvspec, qspec,
                  pl.BlockSpec((2 * SB_TK, 2 * SB_TK), lambda bi, p, i: (0, 0))],
        out_specs=qspec,
        out_shape=jax.ShapeDtypeStruct((b, s, SB_WIDTH), BF16),
        scratch_shapes=[pltpu.VMEM((2, SB_TQ, LANES), BF16),
                        pltpu.VMEM((2, SB_TQ, LANES), F32),
                        pltpu.VMEM((2, SB_TQ, LANES), F32)],
        compiler_params=_params(("arbitrary", "arbitrary", "arbitrary")),
        name="sb_prompt",
    )(bias, q_bf, kt_bf, vt_bf, sg, uu)


GDN_TS = 256
GDN_KCHUNK = 128
GDN_NCH = GDN_TS // GDN_KCHUNK
GDN_INV_BASE = 8


def _chunk_cumsum_matrix(ts):
    j = lax.broadcasted_iota(jnp.int32, (ts, ts), 0)
    s = lax.broadcasted_iota(jnp.int32, (ts, ts), 1)
    return jnp.where((j <= s) & (j // GDN_KCHUNK == s // GDN_KCHUNK), 1.0, 0.0).astype(BF16)


def _l2norm(x):
    return x * lax.rsqrt(jnp.sum(x * x, axis=-1, keepdims=True) + L2_EPS)


def _gdn_out(o, gz, norm_w):
    y = o * lax.rsqrt(jnp.mean(o * o, axis=-1, keepdims=True) + RMS_EPS) * norm_w
    return y * _silu(gz)


def _dot(a, b):
    return jnp.dot(a.astype(BF16), b.astype(BF16), preferred_element_type=F32)


def _dot_nt(a, b):
    return lax.dot_general(a.astype(BF16), b.astype(BF16), _NT, preferred_element_type=F32)


def _gdn_prompt_kernel(qkv_ref, gz_ref, gba_ref, gbat_ref, cw_ref, hrow_ref, hcol_ref, nw_ref,
                       ub_ref, lb_ref, y_ref, conv_ref, rec_ref, xp_ref, st_ref):
    si = pl.program_id(1)
    ts = GDN_TS
    c = GDN_KCHUNK
    dk = GDN_HEAD_DIM
    pad = SUBLANES

    @pl.when(si == 0)
    def _():
        xp_ref[0:pad, :] = jnp.zeros((pad, GDN_CONV_DIM), F32)
        st_ref[...] = jnp.zeros_like(st_ref)

    xp_ref[pad:pad + ts, :] = qkv_ref[...]
    y = xp_ref[pad:pad + ts, :] * cw_ref[CONV_W - 1:CONV_W, :]
    for i in range(CONV_W - 1):
        sh = CONV_W - 1 - i
        y = y + xp_ref[pad - sh:pad - sh + ts, :] * cw_ref[i:i + 1, :]
    qkv = _silu(y)
    tail = xp_ref[ts:ts + pad, :]
    xp_ref[0:pad, :] = tail

    gba = gba_ref[...]
    beta_c = jax.nn.sigmoid(gba)
    g_c = -jnp.exp(hrow_ref[0:1, :]) * _softplus(gba + hrow_ref[1:2, :])
    gbat = gbat_ref[...]
    g_r = -jnp.exp(hcol_ref[:, 0:1]) * _softplus(gbat + hcol_ref[:, 1:2])
    ub = ub_ref[...]
    g_hi, g_lo = _split_bf16(g_c)
    lb = lb_ref[...]
    gc_c = (jnp.dot(lb, g_hi, preferred_element_type=F32)
            + jnp.dot(lb, g_lo, preferred_element_type=F32))
    r_hi, r_lo = _split_bf16(g_r)
    gc_r = (jnp.dot(r_hi, ub, preferred_element_type=F32)
            + jnp.dot(r_lo, ub, preferred_element_type=F32))

    ri = lax.broadcasted_iota(jnp.int32, (c, c), 0)
    ci = lax.broadcasted_iota(jnp.int32, (c, c), 1)
    causal = ci <= ri
    strict = ci < ri
    nw = nw_ref[...]
    gz = gz_ref[...]

    items = [(ch, h) for ch in range(GDN_NCH) for h in range(GDN_HEADS)]
    qn, kn, vv = {}, {}, {}
    for h in range(GDN_HEADS):
        lo = h * dk
        qn[h] = _l2norm(qkv[:, lo:lo + dk]) * dk ** -0.5
        kn[h] = _l2norm(qkv[:, GDN_WIDTH + lo:GDN_WIDTH + lo + dk])
        vv[h] = qkv[:, 2 * GDN_WIDTH + lo:2 * GDN_WIDTH + lo + dk]
    rows = lambda ch: slice(ch * c, (ch + 1) * c)
    kk = {it: kn[it[1]][rows(it[0])] for it in items}
    qq = {it: qn[it[1]][rows(it[0])] for it in items}
    bcol = {it: beta_c[rows(it[0]), it[1]:it[1] + 1] for it in items}
    gcol = {it: gc_c[rows(it[0]), GDN_HEADS + it[1]:GDN_HEADS + it[1] + 1] for it in items}
    grow = {it: gc_r[GDN_HEADS + it[1]:GDN_HEADS + it[1] + 1, rows(it[0])] for it in items}
    decay = {it: jnp.where(causal, jnp.exp(jnp.where(causal, gcol[it] - grow[it], 0.0)), 0.0)
             for it in items}
    kb = {it: kk[it] * bcol[it] for it in items}
    kk_bf = {it: kk[it].astype(BF16) for it in items}
    x = {it: -jnp.where(strict, _dot_nt(kb[it], kk_bf[it]) * decay[it], 0.0) for it in items}
    blk = lambda w: (ri // w) == (ci // w)
    n = {it: jnp.where(blk(GDN_INV_BASE), x[it], 0.0) for it in items}
    p = dict(n)
    for _ in range(GDN_INV_BASE.bit_length() - 2):
        p = {it: _dot(p[it], p[it]) for it in items}
        n = {it: n[it] + p[it] + _dot(n[it], p[it]) for it in items}
    w = GDN_INV_BASE
    while w < c:
        merge = blk(2 * w) & jnp.logical_not(blk(w))
        off = {it: jnp.where(merge, x[it], 0.0) for it in items}
        yy = {it: off[it] + _dot(n[it], off[it]) for it in items}
        n = {it: n[it] + yy[it] + _dot(yy[it], n[it]) for it in items}
        w *= 2
    egc = {it: jnp.exp(gcol[it]) for it in items}
    rhs = {it: jnp.concatenate([vv[it[1]][rows(it[0])] * bcol[it], kb[it] * egc[it]], axis=1)
           for it in items}
    uw = {it: rhs[it] + _dot(n[it], rhs[it]) for it in items}
    intra = {it: jnp.where(causal, _dot_nt(qq[it], kk_bf[it]) * decay[it], 0.0).astype(BF16)
             for it in items}
    qg = {it: (qq[it] * egc[it]).astype(BF16) for it in items}
    glast = {it: grow[it][:, c - 1:c] for it in items}
    kgl = {it: (kk[it] * jnp.exp(glast[it] - gcol[it])).astype(BF16) for it in items}

    state = {h: st_ref[h] for h in range(GDN_HEADS)}
    for ch in range(GDN_NCH):
        its = [(ch, h) for h in range(GDN_HEADS)]
        sb = {it: state[it[1]].astype(BF16) for it in its}
        v_new = {it: (uw[it][:, :dk] - _dot(uw[it][:, dk:], sb[it])).astype(BF16) for it in its}
        o = {it: jnp.dot(qg[it], sb[it], preferred_element_type=F32)
             + jnp.dot(intra[it], v_new[it], preferred_element_type=F32) for it in its}
        for it in its:
            h = it[1]
            state[h] = (state[h] * jnp.exp(glast[it])
                        + lax.dot_general(kgl[it], v_new[it], _TN, preferred_element_type=F32))
            y_ref[rows(ch), h * dk:(h + 1) * dk] = _gdn_out(
                o[it], gz[rows(ch), h * dk:(h + 1) * dk], nw).astype(BF16)
    for h in range(GDN_HEADS):
        st_ref[h] = state[h]

    @pl.when(si == pl.num_programs(1) - 1)
    def _():
        conv_ref[...] = tail[pad - (CONV_W - 1):pad, :]
        rec_ref[...] = st_ref[...]


def _gdn_prompt_call(qkv, gz, gba, gbat, cw, hrow, hcol, nw, ub, lb):
    b, s, _ = qkv.shape
    ns = s // GDN_TS
    const2 = lambda shape: pl.BlockSpec(shape, lambda bi, i: (0, 0))
    return pl.pallas_call(
        _gdn_prompt_kernel,
        grid=(b, ns),
        in_specs=[
            pl.BlockSpec((None, GDN_TS, GDN_CONV_DIM), lambda bi, i: (bi, i, 0)),
            pl.BlockSpec((None, GDN_TS, GDN_WIDTH), lambda bi, i: (bi, i, 0)),
            pl.BlockSpec((None, GDN_TS, LANES), lambda bi, i: (bi, i, 0)),
            pl.BlockSpec((None, SUBLANES, GDN_TS), lambda bi, i: (bi, 0, i)),
            const2((CONV_W, GDN_CONV_DIM)), const2((2, LANES)), const2((SUBLANES, 2)),
            const2((1, GDN_HEAD_DIM)), const2((GDN_TS, GDN_TS)), const2((GDN_TS, GDN_TS)),
        ],
        out_specs=[
            pl.BlockSpec((None, GDN_TS, GDN_WIDTH), lambda bi, i: (bi, i, 0)),
            pl.BlockSpec((None, CONV_W - 1, GDN_CONV_DIM), lambda bi, i: (bi, 0, 0)),
            pl.BlockSpec((None, GDN_HEADS, GDN_HEAD_DIM, GDN_HEAD_DIM), lambda bi, i: (bi, 0, 0, 0)),
        ],
        out_shape=[
            jax.ShapeDtypeStruct((b, s, GDN_WIDTH), BF16),
            jax.ShapeDtypeStruct((b, CONV_W - 1, GDN_CONV_DIM), F32),
            jax.ShapeDtypeStruct((b, GDN_HEADS, GDN_HEAD_DIM, GDN_HEAD_DIM), F32),
        ],
        scratch_shapes=[
            pltpu.VMEM((GDN_TS + SUBLANES, GDN_CONV_DIM), F32),
            pltpu.VMEM((GDN_HEADS, GDN_HEAD_DIM, GDN_HEAD_DIM), F32),
        ],
        compiler_params=_params(("arbitrary", "arbitrary")),
        name="gdn_prompt",
    )(qkv, gz, gba, gbat, cw, hrow, hcol, nw, ub, lb)


def _outproj_kernel(ysb_ref, ygdn_ref, x_ref, gate_ref, wt_ref, wb_ref, g_ref, b_ref, o_ref):
    mixed = (jnp.dot(ysb_ref[...], wt_ref[...], preferred_element_type=F32)
             + jnp.dot(ygdn_ref[...], wb_ref[...], preferred_element_type=F32))
    r = DN_ALPHA * x_ref[...] + gate_ref[...] * mixed
    mu = jnp.mean(r, axis=-1, keepdims=True)
    rc = r - mu
    var = jnp.mean(rc * rc, axis=-1, keepdims=True)
    o_ref[...] = rc * lax.rsqrt(var + LN_EPS) * g_ref[...] + b_ref[...]


def _outproj_call(y_sb, y_gdn, x, gate, w_top, w_bot, ln_g, ln_b, tm, per_row_mod):
    g, t, _ = x.shape
    nt = t // tm
    if per_row_mod:
        gate_spec = pl.BlockSpec((None, tm, D_MODEL), lambda b, i: (b, i, 0))
    else:
        gate_spec = pl.BlockSpec((None, 1, D_MODEL), lambda b, i: (b, 0, 0))
    half = pl.BlockSpec((None, tm, SB_WIDTH), lambda b, i: (b, i, 0))
    full = pl.BlockSpec((None, tm, D_MODEL), lambda b, i: (b, i, 0))
    const = lambda shape: pl.BlockSpec(shape, lambda b, i: (0, 0))
    return pl.pallas_call(
        _outproj_kernel,
        grid=(g, nt),
        in_specs=[half, half, full, gate_spec, const((SB_WIDTH, D_MODEL)), const((GDN_WIDTH, D_MODEL)),
                  const((1, D_MODEL)), const((1, D_MODEL))],
        out_specs=full,
        out_shape=jax.ShapeDtypeStruct((g, t, D_MODEL), F32),
        compiler_params=_params(("arbitrary", "arbitrary")),
        name="out_proj_ln",
    )(y_sb, y_gdn, x, gate, w_top, w_bot, ln_g, ln_b)


def _gdn_step_kernel(qkv_ref, cs_ref, gz_ref, gba_ref, rec_ref, cw_ref, hrow_ref, nw_ref,
                     y_ref, cs_out_ref, rec_out_ref):
    cdim = GDN_CONV_DIM
    x = qkv_ref[...]
    y = x * cw_ref[CONV_W - 1:CONV_W, :]
    for i in range(CONV_W - 1):
        y = y + cs_ref[:, i * cdim:(i + 1) * cdim] * cw_ref[i:i + 1, :]
    qkv = _silu(y)
    for i in range(CONV_W - 2):
        cs_out_ref[:, i * cdim:(i + 1) * cdim] = cs_ref[:, (i + 1) * cdim:(i + 2) * cdim]
    cs_out_ref[:, (CONV_W - 2) * cdim:(CONV_W - 1) * cdim] = x

    gba = gba_ref[...]
    beta = jax.nn.sigmoid(gba)
    g = -jnp.exp(hrow_ref[0:1, :]) * _softplus(gba + hrow_ref[1:2, :])
    eye = (lax.broadcasted_iota(jnp.int32, (GDN_HEAD_DIM, GDN_HEAD_DIM), 0)
           == lax.broadcasted_iota(jnp.int32, (GDN_HEAD_DIM, GDN_HEAD_DIM), 1))

    def column(row):
        return jnp.sum(jnp.where(eye, row, 0.0), axis=1, keepdims=True)

    nw = nw_ref[...]
    gz = gz_ref[...]
    for h in range(GDN_HEADS):
        lo = h * GDN_HEAD_DIM
        hi = lo + GDN_HEAD_DIM
        q_h = _l2norm(qkv[:, lo:hi]) * GDN_HEAD_DIM ** -0.5
        k_h = _l2norm(qkv[:, GDN_WIDTH + lo:GDN_WIDTH + hi])
        v_h = qkv[:, 2 * GDN_WIDTH + lo:2 * GDN_WIDTH + hi]
        s = rec_ref[h] * jnp.exp(g[:, GDN_HEADS + h:GDN_HEADS + h + 1])
        kcol = column(k_h)
        delta = (v_h - jnp.sum(s * kcol, axis=0, keepdims=True)) * beta[:, h:h + 1]
        s = s + kcol * delta
        rec_out_ref[h] = s
        o = jnp.sum(s * column(q_h), axis=0, keepdims=True)
        y_ref[:, lo:hi] = _gdn_out(o, gz[:, lo:hi], nw).astype(BF16)


def _gdn_step_call(layer, qkv, conv_state, gz, gba, state_rec, cw, hrow, nw):
    db = qkv.shape[0]
    row = lambda width: pl.BlockSpec((None, 1, width), lambda b: (b, 0, 0))
    const = lambda shape: pl.BlockSpec(shape, lambda b: (0, 0))
    rec_shape = (GDN_HEADS, GDN_HEAD_DIM, GDN_HEAD_DIM)
    cs_width = (CONV_W - 1) * GDN_CONV_DIM
    return pl.pallas_call(
        _gdn_step_kernel,
        grid=(db,),
        in_specs=[row(GDN_CONV_DIM), row(cs_width), row(GDN_WIDTH), row(LANES),
                  pl.BlockSpec((None, None) + rec_shape, lambda b: (layer, b, 0, 0, 0)),
                  const((CONV_W, GDN_CONV_DIM)), const((2, LANES)), const((1, GDN_HEAD_DIM))],
        out_specs=[row(GDN_WIDTH), row(cs_width),
                   pl.BlockSpec((None,) + rec_shape, lambda b: (b, 0, 0, 0))],
        out_shape=[jax.ShapeDtypeStruct((db, 1, GDN_WIDTH), BF16),
                   jax.ShapeDtypeStruct((db, 1, cs_width), F32),
                   jax.ShapeDtypeStruct((db,) + rec_shape, F32)],
        compiler_params=_params(("arbitrary",)),
        name="gdn_step",
    )(qkv, conv_state, gz, gba, state_rec, cw, hrow, nw)


PAGES_PER_STEP = 8


def _sb_paged_kernel(pt_ref, q_ref, kn_ref, vn_ref, sg_ref, bias_ref, uu_ref, *rest):
    del pt_ref
    np_ = PAGES_PER_STEP
    k_refs = rest[:np_]
    v_refs = rest[np_:2 * np_]
    y_ref, qm_ref, carry_ref, acc_ref = rest[2 * np_:]
    gi = pl.program_id(1)
    head = lax.broadcasted_iota(jnp.int32, (SB_HEADS, SB_WIDTH), 0)
    chan = lax.broadcasted_iota(jnp.int32, (SB_HEADS, SB_WIDTH), 1)
    own = chan // SB_HEAD_DIM == head

    @pl.when(gi == 0)
    def _():
        qm_ref[...] = jnp.where(own, q_ref[...].astype(F32), 0.0)
        carry_ref[...] = jnp.zeros_like(carry_ref)
        acc_ref[...] = jnp.zeros_like(acc_ref)

    qm32 = qm_ref[...]
    qm = qm32.astype(BF16)
    bias = bias_ref[...] * LOG2E
    uu = uu_ref[...]
    carry = carry_ref[...]
    acc = acc_ref[...]
    scores = [_sb_scores(qm, k_refs[i][...].astype(BF16), bias, uu, None) for i in range(np_)]
    for i in reversed(range(np_)):
        carry, pv = _sb_apply(scores[i][0], scores[i][1], carry, v_refs[i][...].astype(BF16), None)
        acc = acc + pv
    carry_ref[...] = carry
    acc_ref[...] = acc

    @pl.when(gi == pl.num_programs(1) - 1)
    def _():
        t_new = kn_ref.shape[0]
        z_new = jnp.sum(qm32 * kn_ref[...], axis=-1, keepdims=True) + bias
        q_idx = lax.broadcasted_iota(jnp.int32, (SB_HEADS, t_new), 1) + (t_new - 1)
        k_idx = lax.broadcasted_iota(jnp.int32, (SB_HEADS, t_new), 1)
        w_new = jnp.where(k_idx < q_idx, jnp.exp2(z_new - _softplus(z_new / LOG2E) * LOG2E), 0.0)
        total = acc + w_new * vn_ref[...]
        o = jnp.sum(jnp.where(own, total, 0.0), axis=0, keepdims=True)
        y_ref[...] = (o * _silu(sg_ref[...])).astype(BF16)


def _sb_paged_call(layer, page_table, q_bf, k_new, v_new, sg, bias_col, uu, cache_k, cache_v):
    db, n_pages = page_table.shape
    np_ = PAGES_PER_STEP
    ng = n_pages // np_
    page = cache_k.shape[2]
    ck = jnp.transpose(cache_k, (0, 1, 3, 4, 2)).reshape(cache_k.shape[0], cache_k.shape[1], SB_WIDTH, page)
    cv = jnp.transpose(cache_v, (0, 1, 3, 4, 2)).reshape(cache_v.shape[0], cache_v.shape[1], SB_WIDTH, page)
    row = pl.BlockSpec((None, 1, SB_WIDTH), lambda b, g, pt: (b, 0, 0))

    def page_spec(i):
        return pl.BlockSpec(
            (None, None, SB_WIDTH, page),
            lambda b, g, pt: (layer, pt[b, (ng - 1 - g) * np_ + i], 0, 0))

    grid_spec = pltpu.PrefetchScalarGridSpec(
        num_scalar_prefetch=1,
        grid=(db, ng),
        in_specs=[row, row, row, row,
                  pl.BlockSpec((SB_HEADS, 1), lambda b, g, pt: (0, 0)),
                  pl.BlockSpec((2 * SB_TK, 2 * SB_TK), lambda b, g, pt: (0, 0))]
                 + [page_spec(i) for i in range(np_)] * 2,
        out_specs=row,
        scratch_shapes=[pltpu.VMEM((SB_HEADS, SB_WIDTH), F32),
                        pltpu.VMEM((SB_HEADS, SB_TK), F32),
                        pltpu.VMEM((SB_HEADS, SB_WIDTH), F32)],
    )
    return pl.pallas_call(
        _sb_paged_kernel,
        grid_spec=grid_spec,
        out_shape=jax.ShapeDtypeStruct((db, 1, SB_WIDTH), BF16),
        compiler_params=_params(("arbitrary", "arbitrary")),
        name="sb_paged",
    )(page_table, q_bf, k_new, v_new, sg, bias_col, uu, *([ck] * np_), *([cv] * np_))


def _pick_tile(t, cap):
    tm = min(t, cap)
    assert t % tm == 0
    return tm


def kernel(x_prompt, x_sample, cache_k, cache_v, page_table, state_conv, state_rec, c_prompt, c_sample,
           w_mod, b_mod, w_in, sb_bias, conv_w, a_log, dt_bias, gdn_norm_w, w_out, ln_g, ln_b):
    b, s, _ = x_prompt.shape
    db, t_new, _ = x_sample.shape
    assert t_new == 1 and s % GDN_TS == 0 and s % SB_TQ == 0
    assert page_table.shape[1] % PAGES_PER_STEP == 0 and cache_k.shape[2] == SB_TK

    mod = _mod_call(jnp.concatenate([c_sample, c_prompt], axis=0), w_mod, b_mod)
    mod_rows = mod.reshape(DEPTH, 3, db + b, 1, D_MODEL)

    uu = _cumsum_matrix()
    ub = _chunk_cumsum_matrix(GDN_TS)
    lb = ub.T
    gate_pad = ((0, 0), (GDN_HEADS, LANES - 2 * GDN_HEADS))

    hp, hs = x_prompt, x_sample.reshape(1, db, D_MODEL)
    kp_l, vp_l, cp_l, rp_l, ks_l, vs_l, cs_l, rs_l = [], [], [], [], [], [], [], []
    for l in range(DEPTH):
        w_bf = jnp.pad(w_in[l], ((0, 0), (0, W_IN_PAD - w_in.shape[2]))).astype(BF16)
        wt_bf = jnp.concatenate(
            [w_in[l][:, COL_K:COL_G], w_in[l][:, COL_BA:COL_BA + SUBLANES]], axis=1).T.astype(BF16)
        w_top = w_out[l][:SB_WIDTH].astype(BF16)
        w_bot = w_out[l][SB_WIDTH:].astype(BF16)
        hrow = jnp.pad(jnp.stack([a_log[l], dt_bias[l]]), gate_pad)
        hcol = hrow[:, :SUBLANES].T
        nw = gdn_norm_w[l].reshape(1, GDN_HEAD_DIM)
        lng = ln_g[l].reshape(1, D_MODEL)
        lnb = ln_b[l].reshape(1, D_MODEL)

        shift, scale, gate = (mod_rows[l, i, db:] for i in range(3))
        tm = _pick_tile(s, 256)
        q_bf, kt, vt, kt_bf, vt_bf, sg, qkv, gz, gba, gbat = _inproj_call(
            hp, shift, scale, w_bf, wt_bf, tm, per_row_mod=False, transposed_kv=True)
        y_sb = _sb_prompt_call(q_bf, kt_bf, vt_bf, sg, sb_bias[l], uu)
        y_gdn, conv_p, rec_p = _gdn_prompt_call(qkv, gz, gba, gbat, conv_w[l], hrow, hcol, nw, ub, lb)
        hp = _outproj_call(y_sb, y_gdn, hp, gate, w_top, w_bot, lng, lnb, _pick_tile(s, 512), False)
        heads_last = lambda a: jnp.transpose(a.reshape(b, SB_HEADS, SB_HEAD_DIM, s), (0, 3, 1, 2))
        kp_l.append(heads_last(kt))
        vp_l.append(heads_last(vt))
        cp_l.append(conv_p)
        rp_l.append(rec_p)

        shift, scale, gate = (mod[l, i, :db].reshape(1, db, D_MODEL) for i in range(3))
        q_bf, k, v, sg, qkv, gz, gba = _inproj_call(
            hs, shift, scale, w_bf, wt_bf, db, per_row_mod=True, transposed_kv=False)
        as_rows = lambda a: a.reshape(db, 1, a.shape[-1])
        y_sb = _sb_paged_call(l, page_table, as_rows(q_bf), as_rows(k), as_rows(v), as_rows(sg),
                              sb_bias[l].reshape(SB_HEADS, 1), uu, cache_k, cache_v)
        y_gdn, conv_s, rec_s = _gdn_step_call(
            l, as_rows(qkv), state_conv[l].reshape(db, 1, (CONV_W - 1) * GDN_CONV_DIM),
            as_rows(gz), as_rows(gba), state_rec, conv_w[l], hrow, nw)
        hs = _outproj_call(y_sb.reshape(1, db, SB_WIDTH), y_gdn.reshape(1, db, GDN_WIDTH), hs, gate,
                           w_top, w_bot, lng, lnb, db, True)
        ks_l.append(k.reshape(db, 1, SB_HEADS, SB_HEAD_DIM))
        vs_l.append(v.reshape(db, 1, SB_HEADS, SB_HEAD_DIM))
        cs_l.append(conv_s.reshape(db, CONV_W - 1, GDN_CONV_DIM))
        rs_l.append(rec_s)

    return (hp, hs.reshape(db, 1, D_MODEL), jnp.stack(kp_l), jnp.stack(vp_l), jnp.stack(cp_l),
            jnp.stack(rp_l), jnp.stack(ks_l), jnp.stack(vs_l), jnp.stack(cs_l), jnp.stack(rs_l))
```

```python
import functools

import jax
import jax.numpy as jnp
from jax import lax
from jax.experimental import pallas as pl
from jax.experimental.pallas import tpu as pltpu

F32 = jnp.float32
BF16 = jnp.bfloat16

D_MODEL = 1024
DEPTH = 2
SB_HEADS = 8
SB_HEAD_DIM = 64
SB_WIDTH = SB_HEADS * SB_HEAD_DIM
SB_SCALE = SB_HEAD_DIM ** -0.5
GDN_HEADS = 4
GDN_HEAD_DIM = 128
GDN_WIDTH = GDN_HEADS * GDN_HEAD_DIM
GDN_CONV_DIM = 3 * GDN_WIDTH
CONV_W = 4
DN_ALPHA = (2 * DEPTH) ** 0.25
LN_EPS = 1e-5
RMS_EPS = 1e-6
L2_EPS = 1e-6

LANES = 128
SUBLANES = 8
COL_Q, COL_K, COL_V, COL_G = 0, SB_WIDTH, 2 * SB_WIDTH, 3 * SB_WIDTH
COL_QKV = 4 * SB_WIDTH
COL_Z = COL_QKV + GDN_CONV_DIM
COL_BA = COL_Z + GDN_WIDTH
W_IN_PAD = COL_BA + LANES
VMEM_LIMIT = 56 * 1024 * 1024

_NT = (((1,), (1,)), ((), ()))
_TN = (((0,), (0,)), ((), ()))


def _silu(x):
    return x * jax.nn.sigmoid(x)


def _softplus(x):
    return jnp.maximum(x, 0.0) + jnp.log1p(jnp.exp(-jnp.abs(x)))


def _split_bf16(x):
    hi = x.astype(BF16)
    lo = (x - hi.astype(F32)).astype(BF16)
    return hi, lo


def _params(sem):
    return pltpu.CompilerParams(dimension_semantics=sem, vmem_limit_bytes=VMEM_LIMIT)


def _mod_kernel(c_ref, w_ref, b_ref, o_ref):
    a = _silu(c_ref[...]).astype(BF16)
    o_ref[...] = jnp.dot(a, w_ref[...].astype(BF16), preferred_element_type=F32) + b_ref[...]


def _mod_call(c_all, w_mod, b_mod):
    n = c_all.shape[0]
    b4 = b_mod.reshape(DEPTH, 3, 1, D_MODEL)
    return pl.pallas_call(
        _mod_kernel,
        grid=(DEPTH, 3),
        in_specs=[
            pl.BlockSpec((n, D_MODEL), lambda l, p: (0, 0)),
            pl.BlockSpec((None, D_MODEL, D_MODEL), lambda l, p: (l, 0, p)),
            pl.BlockSpec((None, None, 1, D_MODEL), lambda l, p: (l, p, 0, 0)),
        ],
        out_specs=pl.BlockSpec((None, None, n, D_MODEL), lambda l, p: (l, p, 0, 0)),
        out_shape=jax.ShapeDtypeStruct((DEPTH, 3, n, D_MODEL), F32),
        compiler_params=_params(("arbitrary", "arbitrary")),
        name="adaln_mod",
    )(c_all, w_mod, b4)


SB_TQ = 1024
SB_UNROLL = 4
SB_TK = 128
LOG2E = 1.4426950408889634


def _inproj_kernel(transposed_kv, n_aliased, x_ref, shift_ref, scale_ref, w_ref, wt_ref, *refs):
    out_refs = refs[n_aliased:]
    u = (x_ref[...] * (1.0 + scale_ref[...]) + shift_ref[...]).astype(BF16)

    def seg(lo, width):
        return jnp.dot(u, w_ref[:, lo:lo + width], preferred_element_type=F32)

    if transposed_kv:
        q_ref, kt_ref, vt_ref, ktb_ref, vtb_ref, sg_ref, qkv_ref, gz_ref, gba_ref, gbat_ref = out_refs
        t = lax.dot_general(wt_ref[...], u, _NT, preferred_element_type=F32)
        kt = t[:SB_WIDTH]
        vt = t[SB_WIDTH:2 * SB_WIDTH]
        kt_ref[...] = kt
        vt_ref[...] = vt
        for j in range(u.shape[0] // SB_TK):
            ktb_ref[j] = kt[:, j * SB_TK:(j + 1) * SB_TK].astype(BF16)
            vtb_ref[j] = vt[:, j * SB_TK:(j + 1) * SB_TK].astype(BF16)
        gbat_ref[...] = t[2 * SB_WIDTH:]
    else:
        q_ref, k_ref, v_ref, sg_ref, qkv_ref, gz_ref, gba_ref = out_refs
        k_ref[...] = seg(COL_K, SB_WIDTH)
        v_ref[...] = seg(COL_V, SB_WIDTH)
    q_ref[...] = (seg(COL_Q, SB_WIDTH) * (SB_SCALE * LOG2E)).astype(BF16)
    sg_ref[...] = seg(COL_G, SB_WIDTH)
    for j in range(3):
        qkv_ref[:, j * GDN_WIDTH:(j + 1) * GDN_WIDTH] = seg(COL_QKV + j * GDN_WIDTH, GDN_WIDTH)
    gz_ref[...] = seg(COL_Z, GDN_WIDTH)
    gba_ref[...] = seg(COL_BA, LANES)


def _inproj_call(x, shift, scale, w_bf, wt_bf, tm, per_row_mod, transposed_kv, layer=0, kv_buffers=()):
    g, t, _ = x.shape
    nt = t // tm
    if per_row_mod:
        mod_spec = pl.BlockSpec((None, tm, D_MODEL), lambda b, i: (b, i, 0))
    else:
        mod_spec = pl.BlockSpec((None, 1, D_MODEL), lambda b, i: (b, 0, 0))

    def tok(width, dtype):
        return (pl.BlockSpec((None, tm, width), lambda b, i: (b, i, 0)),
                jax.ShapeDtypeStruct((g, t, width), dtype))

    def chan(rows):
        return (pl.BlockSpec((None, rows, tm), lambda b, i: (b, 0, i)),
                jax.ShapeDtypeStruct((g, rows, t), F32))

    def blocks():
        return (pl.BlockSpec((None, tm // SB_TK, SB_WIDTH, SB_TK), lambda b, i: (b, i, 0, 0)),
                jax.ShapeDtypeStruct((g, t // SB_TK, SB_WIDTH, SB_TK), BF16))

    def layered():
        return (pl.BlockSpec((None, None, SB_WIDTH, tm), lambda b, i: (layer, b, 0, i)),
                jax.ShapeDtypeStruct((DEPTH, g, SB_WIDTH, t), F32))

    tail = [tok(SB_WIDTH, F32), tok(GDN_CONV_DIM, F32), tok(GDN_WIDTH, F32), tok(LANES, F32)]
    if transposed_kv:
        outs = [tok(SB_WIDTH, BF16), layered(), layered(), blocks(), blocks()] + tail
        outs.append(chan(SUBLANES))
    else:
        outs = [tok(SB_WIDTH, BF16), tok(SB_WIDTH, F32), tok(SB_WIDTH, F32)] + tail
    n_in = 5
    return pl.pallas_call(
        functools.partial(_inproj_kernel, transposed_kv, len(kv_buffers)),
        grid=(g, nt),
        in_specs=[
            pl.BlockSpec((None, tm, D_MODEL), lambda b, i: (b, i, 0)),
            mod_spec, mod_spec,
            pl.BlockSpec((D_MODEL, W_IN_PAD), lambda b, i: (0, 0)),
            pl.BlockSpec(wt_bf.shape, lambda b, i: (0, 0)),
        ] + [pl.BlockSpec(memory_space=pl.ANY)] * len(kv_buffers),
        out_specs=[o[0] for o in outs],
        out_shape=[o[1] for o in outs],
        input_output_aliases={n_in + j: 1 + j for j in range(len(kv_buffers))},
        compiler_params=_params(("arbitrary", "arbitrary")),
        name="in_proj",
    )(x, shift, scale, w_bf, wt_bf, *kv_buffers)


def _cumsum_matrix():
    j = lax.broadcasted_iota(jnp.int32, (2 * SB_TK, 2 * SB_TK), 0) % SB_TK
    s = lax.broadcasted_iota(jnp.int32, (2 * SB_TK, 2 * SB_TK), 1)
    return jnp.where((s >= SB_TK) | (j >= s), 1.0, 0.0).astype(BF16)


def _neg_abs(x):
    bits = lax.bitcast_convert_type(x, jnp.uint32) | jnp.uint32(0x80000000)
    return lax.bitcast_convert_type(bits, F32)


def _softplus2(z):
    return jnp.maximum(z, 0.0) + jnp.log(1.0 + jnp.exp2(_neg_abs(z))) * LOG2E


def _mask_leading(x, allowed):
    if allowed is None:
        return x
    n = allowed.shape[0]
    head = jnp.where(allowed, x[:n], 0.0)
    return head if x.shape[0] == n else jnp.concatenate([head, x[n:]], axis=0)


def _sb_scores(qh, kt_blk, bias2, uu, allowed):
    z = jnp.dot(qh, kt_blk, preferred_element_type=F32) + bias2
    hi, lo = _split_bf16(_mask_leading(_softplus2(z), allowed))
    return z, jnp.dot(jnp.concatenate([hi, lo], axis=1), uu, preferred_element_type=F32)


def _sb_apply(z, cs, carry, vt_blk, allowed):
    tk = vt_blk.shape[1]
    w = _mask_leading(jnp.exp2(z - carry - cs[:, :tk]), allowed)
    pv = lax.dot_general(w.astype(BF16), vt_blk, _NT, preferred_element_type=F32)
    return carry + cs[:, tk:], pv


def _sb_prompt_kernel(bias_ref, q_ref, k_ref, v_ref, sg_ref, uu_ref, y_ref, qm_ref, carry_ref, acc_ref):
    pair = pl.program_id(1)
    qi = pl.program_id(2)
    nsub = SB_TQ // SB_TK
    q = q_ref[...]
    uu = uu_ref[...]
    lane = lax.broadcasted_iota(jnp.int32, (SB_TQ, LANES), 1)
    for h in range(2):
        qm_ref[h] = q * jnp.where(lane // SB_HEAD_DIM == h, 1.0, 0.0).astype(BF16)
    carry_ref[...] = jnp.zeros_like(carry_ref)
    acc_ref[...] = jnp.zeros_like(acc_ref)
    bias2 = [bias_ref[2 * pair + h] * LOG2E for h in range(2)]

    row = lax.broadcasted_iota(jnp.int32, (SB_TK, SB_TK), 0)
    col = lax.broadcasted_iota(jnp.int32, (SB_TK, SB_TK), 1)
    causal = col < row

    def sweep(tasks):
        masks = [causal if on_diagonal else None for _, _, on_diagonal in tasks]
        scores = [[_sb_scores(qm_ref[h, first:, :], k_ref[kb], bias2[h], uu, allowed) for h in range(2)]
                  for (kb, first, _), allowed in zip(tasks, masks)]
        for (kb, first, _), allowed, per_head in zip(tasks, masks, scores):
            vt = v_ref[kb]
            for h, (z, cs) in enumerate(per_head):
                carry, pv = _sb_apply(z, cs, carry_ref[h, first:, :], vt, allowed)
                carry_ref[h, first:, :] = carry
                acc_ref[h, first:, :] += pv

    for j0 in reversed(range(0, nsub, SB_UNROLL)):
        sweep([(qi * nsub + j, j * SB_TK, True) for j in reversed(range(j0, j0 + SB_UNROLL))])

    def body(i, _):
        kb = qi * nsub - 1 - SB_UNROLL * i
        sweep([(kb - u, 0, False) for u in range(SB_UNROLL)])
        return 0

    lax.fori_loop(0, qi * (nsub // SB_UNROLL), body, 0)
    o = jnp.where(lane < SB_HEAD_DIM, acc_ref[0], acc_ref[1])
    y_ref[...] = (o * _silu(sg_ref[...])).astype(BF16)


def _sb_prompt_call(q_bf, kt_bf, vt_bf, sg, bias, uu):
    b, s, _ = q_bf.shape
    nq = s // SB_TQ
    npair = SB_WIDTH // LANES
    qspec = pl.BlockSpec((None, SB_TQ, LANES), lambda bi, p, i: (bi, i, p))
    kvspec = pl.BlockSpec((None, s // SB_TK, LANES, SB_TK), lambda bi, p, i: (bi, 0, p, 0))
    return pl.pallas_call(
        _sb_prompt_kernel,
        grid=(b, npair, nq),
        in_specs=[pl.BlockSpec(memory_space=pltpu.SMEM), qspec, kvspec, kvspec, qspec,
                  pl.BlockSpec((2 * SB_TK, 2 * SB_TK), lambda bi, p, i: (0, 0))],
        out_specs=qspec,
        out_shape=jax.ShapeDtypeStruct((b, s, SB_WIDTH), BF16),
        scratch_shapes=[pltpu.VMEM((2, SB_TQ, LANES), BF16),
                        pltpu.VMEM((2, SB_TQ, LANES), F32),
                        pltpu.VMEM((2, SB_TQ, LANES), F32)],
        compiler_params=_params(("arbitrary", "arbitrary", "arbitrary")),
        name="sb_prompt",
    )(bias, q_bf, kt_bf, vt_bf, sg, uu)


GDN_TS = 256
GDN_KCHUNK = 128
GDN_NCH = GDN_TS // GDN_KCHUNK
GDN_INV_BASE = 8


def _chunk_cumsum_matrix(ts):
    j = lax.broadcasted_iota(jnp.int32, (ts, ts), 0)
    s = lax.broadcasted_iota(jnp.int32, (ts, ts), 1)
    return jnp.where((j <= s) & (j // GDN_KCHUNK == s // GDN_KCHUNK), 1.0, 0.0).astype(BF16)


def _l2norm(x):
    return x * lax.rsqrt(jnp.sum(x * x, axis=-1, keepdims=True) + L2_EPS)


def _gdn_out(o, gz, norm_w):
    y = o * lax.rsqrt(jnp.mean(o * o, axis=-1, keepdims=True) + RMS_EPS) * norm_w
    return y * _silu(gz)


def _dot(a, b):
    return jnp.dot(a.astype(BF16), b.astype(BF16), preferred_element_type=F32)


def _dot_nt(a, b):
    return lax.dot_general(a.astype(BF16), b.astype(BF16), _NT, preferred_element_type=F32)


def _gdn_prompt_kernel(qkv_ref, gz_ref, gba_ref, gbat_ref, cw_ref, hrow_ref, hcol_ref, nw_ref,
                       ub_ref, lb_ref, y_ref, conv_ref, rec_ref, xp_ref, st_ref):
    si = pl.program_id(1)
    ts = GDN_TS
    c = GDN_KCHUNK
    dk = GDN_HEAD_DIM
    pad = SUBLANES

    @pl.when(si == 0)
    def _():
        xp_ref[0:pad, :] = jnp.zeros((pad, GDN_CONV_DIM), F32)
        st_ref[...] = jnp.zeros_like(st_ref)

    xp_ref[pad:pad + ts, :] = qkv_ref[...]
    y = xp_ref[pad:pad + ts, :] * cw_ref[CONV_W - 1:CONV_W, :]
    for i in range(CONV_W - 1):
        sh = CONV_W - 1 - i
        y = y + xp_ref[pad - sh:pad - sh + ts, :] * cw_ref[i:i + 1, :]
    qkv = _silu(y)
    tail = xp_ref[ts:ts + pad, :]
    xp_ref[0:pad, :] = tail

    gba = gba_ref[...]
    beta_c = jax.nn.sigmoid(gba)
    g_c = -jnp.exp(hrow_ref[0:1, :]) * _softplus(gba + hrow_ref[1:2, :])
    gbat = gbat_ref[...]
    g_r = -jnp.exp(hcol_ref[:, 0:1]) * _softplus(gbat + hcol_ref[:, 1:2])
    ub = ub_ref[...]
    g_hi, g_lo = _split_bf16(g_c)
    lb = lb_ref[...]
    gc_c = (jnp.dot(lb, g_hi, preferred_element_type=F32)
            + jnp.dot(lb, g_lo, preferred_element_type=F32))
    r_hi, r_lo = _split_bf16(g_r)
    gc_r = (jnp.dot(r_hi, ub, preferred_element_type=F32)
            + jnp.dot(r_lo, ub, preferred_element_type=F32))

    ri = lax.broadcasted_iota(jnp.int32, (c, c), 0)
    ci = lax.broadcasted_iota(jnp.int32, (c, c), 1)
    causal = ci <= ri
    strict = ci < ri
    nw = nw_ref[...]
    gz = gz_ref[...]

    items = [(ch, h) for ch in range(GDN_NCH) for h in range(GDN_HEADS)]
    qn, kn, vv = {}, {}, {}
    for h in range(GDN_HEADS):
        lo = h * dk
        qn[h] = _l2norm(qkv[:, lo:lo + dk]) * dk ** -0.5
        kn[h] = _l2norm(qkv[:, GDN_WIDTH + lo:GDN_WIDTH + lo + dk])
        vv[h] = qkv[:, 2 * GDN_WIDTH + lo:2 * GDN_WIDTH + lo + dk]
    rows = lambda ch: slice(ch * c, (ch + 1) * c)
    kk = {it: kn[it[1]][rows(it[0])] for it in items}
    qq = {it: qn[it[1]][rows(it[0])] for it in items}
    bcol = {it: beta_c[rows(it[0]), it[1]:it[1] + 1] for it in items}
    gcol = {it: gc_c[rows(it[0]), GDN_HEADS + it[1]:GDN_HEADS + it[1] + 1] for it in items}
    grow = {it: gc_r[GDN_HEADS + it[1]:GDN_HEADS + it[1] + 1, rows(it[0])] for it in items}
    decay = {it: jnp.where(causal, jnp.exp(jnp.where(causal, gcol[it] - grow[it], 0.0)), 0.0)
             for it in items}
    kb = {it: kk[it] * bcol[it] for it in items}
    kk_bf = {it: kk[it].astype(BF16) for it in items}
    x = {it: -jnp.where(strict, _dot_nt(kb[it], kk_bf[it]) * decay[it], 0.0) for it in items}
    blk = lambda w: (ri // w) == (ci // w)
    n = {it: jnp.where(blk(GDN_INV_BASE), x[it], 0.0) for it in items}
    p = dict(n)
    for _ in range(GDN_INV_BASE.bit_length() - 2):
        p = {it: _dot(p[it], p[it]) for it in items}
        n = {it: n[it] + p[it] + _dot(n[it], p[it]) for it in items}
    w = GDN_INV_BASE
    while w < c:
        merge = blk(2 * w) & jnp.logical_not(blk(w))
        off = {it: jnp.where(merge, x[it], 0.0) for it in items}
        yy = {it: off[it] + _dot(n[it], off[it]) for it in items}
        n = {it: n[it] + yy[it] + _dot(yy[it], n[it]) for it in items}
        w *= 2
    egc = {it: jnp.exp(gcol[it]) for it in items}
    rhs = {it: jnp.concatenate([vv[it[1]][rows(it[0])] * bcol[it], kb[it] * egc[it]], axis=1)
           for it in items}
    uw = {it: rhs[it] + _dot(n[it], rhs[it]) for it in items}
    intra = {it: jnp.where(causal, _dot_nt(qq[it], kk_bf[it]) * decay[it], 0.0).astype(BF16)
             for it in items}
    qg = {it: (qq[it] * egc[it]).astype(BF16) for it in items}
    glast = {it: grow[it][:, c - 1:c] for it in items}
    kgl = {it: (kk[it] * jnp.exp(glast[it] - gcol[it])).astype(BF16) for it in items}

    state = {h: st_ref[h] for h in range(GDN_HEADS)}
    for ch in range(GDN_NCH):
        its = [(ch, h) for h in range(GDN_HEADS)]
        sb = {it: state[it[1]].astype(BF16) for it in its}
        v_new = {it: (uw[it][:, :dk] - _dot(uw[it][:, dk:], sb[it])).astype(BF16) for it in its}
        o = {it: jnp.dot(qg[it], sb[it], preferred_element_type=F32)
             + jnp.dot(intra[it], v_new[it], preferred_element_type=F32) for it in its}
        for it in its:
            h = it[1]
            state[h] = (state[h] * jnp.exp(glast[it])
                        + lax.dot_general(kgl[it], v_new[it], _TN, preferred_element_type=F32))
            y_ref[rows(ch), h * dk:(h + 1) * dk] = _gdn_out(
                o[it], gz[rows(ch), h * dk:(h + 1) * dk], nw).astype(BF16)
    for h in range(GDN_HEADS):
        st_ref[h] = state[h]

    @pl.when(si == pl.num_programs(1) - 1)
    def _():
        conv_ref[...] = tail[pad - (CONV_W - 1):pad, :]
        rec_ref[...] = st_ref[...]


def _gdn_prompt_call(qkv, gz, gba, gbat, cw, hrow, hcol, nw, ub, lb):
    b, s, _ = qkv.shape
    ns = s // GDN_TS
    const2 = lambda shape: pl.BlockSpec(shape, lambda bi, i: (0, 0))
    return pl.pallas_call(
        _gdn_prompt_kernel,
        grid=(b, ns),
        in_specs=[
            pl.BlockSpec((None, GDN_TS, GDN_CONV_DIM), lambda bi, i: (bi, i, 0)),
            pl.BlockSpec((None, GDN_TS, GDN_WIDTH), lambda bi, i: (bi, i, 0)),
            pl.BlockSpec((None, GDN_TS, LANES), lambda bi, i: (bi, i, 0)),
            pl.BlockSpec((None, SUBLANES, GDN_TS), lambda bi, i: (bi, 0, i)),
            const2((CONV_W, GDN_CONV_DIM)), const2((2, LANES)), const2((SUBLANES, 2)),
            const2((1, GDN_HEAD_DIM)), const2((GDN_TS, GDN_TS)), const2((GDN_TS, GDN_TS)),
        ],
        out_specs=[
            pl.BlockSpec((None, GDN_TS, GDN_WIDTH), lambda bi, i: (bi, i, 0)),
            pl.BlockSpec((None, CONV_W - 1, GDN_CONV_DIM), lambda bi, i: (bi, 0, 0)),
            pl.BlockSpec((None, GDN_HEADS, GDN_HEAD_DIM, GDN_HEAD_DIM), lambda bi, i: (bi, 0, 0, 0)),
        ],
        out_shape=[
            jax.ShapeDtypeStruct((b, s, GDN_WIDTH), BF16),
            jax.ShapeDtypeStruct((b, CONV_W - 1, GDN_CONV_DIM), F32),
            jax.ShapeDtypeStruct((b, GDN_HEADS, GDN_HEAD_DIM, GDN_HEAD_DIM), F32),
        ],
        scratch_shapes=[
            pltpu.VMEM((GDN_TS + SUBLANES, GDN_CONV_DIM), F32),
            pltpu.VMEM((GDN_HEADS, GDN_HEAD_DIM, GDN_HEAD_DIM), F32),
        ],
        compiler_params=_params(("arbitrary", "arbitrary")),
        name="gdn_prompt",
    )(qkv, gz, gba, gbat, cw, hrow, hcol, nw, ub, lb)


def _outproj_kernel(ysb_ref, ygdn_ref, x_ref, gate_ref, wt_ref, wb_ref, g_ref, b_ref, o_ref):
    mixed = (jnp.dot(ysb_ref[...], wt_ref[...], preferred_element_type=F32)
             + jnp.dot(ygdn_ref[...], wb_ref[...], preferred_element_type=F32))
    r = DN_ALPHA * x_ref[...] + gate_ref[...] * mixed
    mu = jnp.mean(r, axis=-1, keepdims=True)
    rc = r - mu
    var = jnp.mean(rc * rc, axis=-1, keepdims=True)
    o_ref[...] = rc * lax.rsqrt(var + LN_EPS) * g_ref[...] + b_ref[...]


def _outproj_call(y_sb, y_gdn, x, gate, w_top, w_bot, ln_g, ln_b, tm, per_row_mod):
    g, t, _ = x.shape
    nt = t // tm
    if per_row_mod:
        gate_spec = pl.BlockSpec((None, tm, D_MODEL), lambda b, i: (b, i, 0))
    else:
        gate_spec = pl.BlockSpec((None, 1, D_MODEL), lambda b, i: (b, 0, 0))
    half = pl.BlockSpec((None, tm, SB_WIDTH), lambda b, i: (b, i, 0))
    full = pl.BlockSpec((None, tm, D_MODEL), lambda b, i: (b, i, 0))
    const = lambda shape: pl.BlockSpec(shape, lambda b, i: (0, 0))
    return pl.pallas_call(
        _outproj_kernel,
        grid=(g, nt),
        in_specs=[half, half, full, gate_spec, const((SB_WIDTH, D_MODEL)), const((GDN_WIDTH, D_MODEL)),
                  const((1, D_MODEL)), const((1, D_MODEL))],
        out_specs=full,
        out_shape=jax.ShapeDtypeStruct((g, t, D_MODEL), F32),
        compiler_params=_params(("arbitrary", "arbitrary")),
        name="out_proj_ln",
    )(y_sb, y_gdn, x, gate, w_top, w_bot, ln_g, ln_b)


def _gdn_step_kernel(qkv_ref, cs_ref, gz_ref, gba_ref, rec_ref, cw_ref, hrow_ref, nw_ref,
                     y_ref, cs_out_ref, rec_out_ref):
    cdim = GDN_CONV_DIM
    x = qkv_ref[...]
    y = x * cw_ref[CONV_W - 1:CONV_W, :]
    for i in range(CONV_W - 1):
        y = y + cs_ref[:, i * cdim:(i + 1) * cdim] * cw_ref[i:i + 1, :]
    qkv = _silu(y)
    for i in range(CONV_W - 2):
        cs_out_ref[:, i * cdim:(i + 1) * cdim] = cs_ref[:, (i + 1) * cdim:(i + 2) * cdim]
    cs_out_ref[:, (CONV_W - 2) * cdim:(CONV_W - 1) * cdim] = x

    gba = gba_ref[...]
    beta = jax.nn.sigmoid(gba)
    g = -jnp.exp(hrow_ref[0:1, :]) * _softplus(gba + hrow_ref[1:2, :])
    eye = (lax.broadcasted_iota(jnp.int32, (GDN_HEAD_DIM, GDN_HEAD_DIM), 0)
           == lax.broadcasted_iota(jnp.int32, (GDN_HEAD_DIM, GDN_HEAD_DIM), 1))

    def column(row):
        return jnp.sum(jnp.where(eye, row, 0.0), axis=1, keepdims=True)

    nw = nw_ref[...]
    gz = gz_ref[...]
    for h in range(GDN_HEADS):
        lo = h * GDN_HEAD_DIM
        hi = lo + GDN_HEAD_DIM
        q_h = _l2norm(qkv[:, lo:hi]) * GDN_HEAD_DIM ** -0.5
        k_h = _l2norm(qkv[:, GDN_WIDTH + lo:GDN_WIDTH + hi])
        v_h = qkv[:, 2 * GDN_WIDTH + lo:2 * GDN_WIDTH + hi]
        s = rec_ref[h] * jnp.exp(g[:, GDN_HEADS + h:GDN_HEADS + h + 1])
        kcol = column(k_h)
        delta = (v_h - jnp.sum(s * kcol, axis=0, keepdims=True)) * beta[:, h:h + 1]
        s = s + kcol * delta
        rec_out_ref[h] = s
        o = jnp.sum(s * column(q_h), axis=0, keepdims=True)
        y_ref[:, lo:hi] = _gdn_out(o, gz[:, lo:hi], nw).astype(BF16)


def _gdn_step_call(layer, qkv, conv_state, gz, gba, state_rec, cw, hrow, nw):
    db = qkv.shape[0]
    row = lambda width: pl.BlockSpec((None, 1, width), lambda b: (b, 0, 0))
    const = lambda shape: pl.BlockSpec(shape, lambda b: (0, 0))
    rec_shape = (GDN_HEADS, GDN_HEAD_DIM, GDN_HEAD_DIM)
    cs_width = (CONV_W - 1) * GDN_CONV_DIM
    return pl.pallas_call(
        _gdn_step_kernel,
        grid=(db,),
        in_specs=[row(GDN_CONV_DIM), row(cs_width), row(GDN_WIDTH), row(LANES),
                  pl.BlockSpec((None, None) + rec_shape, lambda b: (layer, b, 0, 0, 0)),
                  const((CONV_W, GDN_CONV_DIM)), const((2, LANES)), const((1, GDN_HEAD_DIM))],
        out_specs=[row(GDN_WIDTH), row(cs_width),
                   pl.BlockSpec((None,) + rec_shape, lambda b: (b, 0, 0, 0))],
        out_shape=[jax.ShapeDtypeStruct((db, 1, GDN_WIDTH), BF16),
                   jax.ShapeDtypeStruct((db, 1, cs_width), F32),
                   jax.ShapeDtypeStruct((db,) + rec_shape, F32)],
        compiler_params=_params(("arbitrary",)),
        name="gdn_step",
    )(qkv, conv_state, gz, gba, state_rec, cw, hrow, nw)


PAGES_PER_STEP = 16


def _sb_paged_kernel(pt_ref, q_ref, kn_ref, vn_ref, sg_ref, bias_ref, uu_ref, *rest):
    del pt_ref
    np_ = PAGES_PER_STEP
    k_refs = rest[:np_]
    v_refs = rest[np_:2 * np_]
    y_ref, qm_ref, carry_ref, acc_ref = rest[2 * np_:]
    gi = pl.program_id(1)
    head = lax.broadcasted_iota(jnp.int32, (SB_HEADS, SB_WIDTH), 0)
    chan = lax.broadcasted_iota(jnp.int32, (SB_HEADS, SB_WIDTH), 1)
    own = chan // SB_HEAD_DIM == head

    @pl.when(gi == 0)
    def _():
        qm_ref[...] = jnp.where(own, q_ref[...].astype(F32), 0.0)
        carry_ref[...] = jnp.zeros_like(carry_ref)
        acc_ref[...] = jnp.zeros_like(acc_ref)

    qm32 = qm_ref[...]
    qm = qm32.astype(BF16)
    bias = bias_ref[...] * LOG2E
    tk = SB_TK
    kt_all = jnp.concatenate([r[...].astype(BF16) for r in k_refs], axis=1)
    z = jnp.dot(qm, kt_all, preferred_element_type=F32) + bias
    sp = _softplus2(z)
    hi, lo = _split_bf16(jnp.concatenate([sp[:, i * tk:(i + 1) * tk] for i in range(np_)], axis=0))
    cs = jnp.dot(jnp.concatenate([hi, lo], axis=1), uu_ref[...], preferred_element_type=F32)
    carry = carry_ref[...]
    ws = [None] * np_
    for i in reversed(range(np_)):
        cs_i = cs[i * SB_HEADS:(i + 1) * SB_HEADS]
        ws[i] = jnp.exp2(z[:, i * tk:(i + 1) * tk] - carry - cs_i[:, :tk])
        carry = carry + cs_i[:, tk:]
    carry_ref[...] = carry
    vt_all = jnp.concatenate([r[...].astype(BF16) for r in v_refs], axis=1)
    acc = acc_ref[...] + lax.dot_general(jnp.concatenate(ws, axis=1).astype(BF16), vt_all, _NT,
                                         preferred_element_type=F32)
    acc_ref[...] = acc

    @pl.when(gi == pl.num_programs(1) - 1)
    def _():
        t_new = kn_ref.shape[0]
        z_new = jnp.sum(qm32 * kn_ref[...], axis=-1, keepdims=True) + bias
        q_idx = lax.broadcasted_iota(jnp.int32, (SB_HEADS, t_new), 1) + (t_new - 1)
        k_idx = lax.broadcasted_iota(jnp.int32, (SB_HEADS, t_new), 1)
        w_new = jnp.where(k_idx < q_idx, jnp.exp2(z_new - _softplus(z_new / LOG2E) * LOG2E), 0.0)
        total = acc + w_new * vn_ref[...]
        o = jnp.sum(jnp.where(own, total, 0.0), axis=0, keepdims=True)
        y_ref[...] = (o * _silu(sg_ref[...])).astype(BF16)


def _sb_paged_call(layer, page_table, q_bf, k_new, v_new, sg, bias_col, uu, cache_k, cache_v):
    db, n_pages = page_table.shape
    np_ = PAGES_PER_STEP
    ng = n_pages // np_
    page = cache_k.shape[2]
    ck = jnp.transpose(cache_k, (0, 1, 3, 4, 2)).reshape(cache_k.shape[0], cache_k.shape[1], SB_WIDTH, page)
    cv = jnp.transpose(cache_v, (0, 1, 3, 4, 2)).reshape(cache_v.shape[0], cache_v.shape[1], SB_WIDTH, page)
    row = pl.BlockSpec((None, 1, SB_WIDTH), lambda b, g, pt: (b, 0, 0))

    def page_spec(i):
        return pl.BlockSpec(
            (None, None, SB_WIDTH, page),
            lambda b, g, pt: (layer, pt[b, (ng - 1 - g) * np_ + i], 0, 0))

    grid_spec = pltpu.PrefetchScalarGridSpec(
        num_scalar_prefetch=1,
        grid=(db, ng),
        in_specs=[row, row, row, row,
                  pl.BlockSpec((SB_HEADS, 1), lambda b, g, pt: (0, 0)),
                  pl.BlockSpec((2 * SB_TK, 2 * SB_TK), lambda b, g, pt: (0, 0))]
                 + [page_spec(i) for i in range(np_)] * 2,
        out_specs=row,
        scratch_shapes=[pltpu.VMEM((SB_HEADS, SB_WIDTH), F32),
                        pltpu.VMEM((SB_HEADS, SB_TK), F32),
                        pltpu.VMEM((SB_HEADS, SB_WIDTH), F32)],
    )
    return pl.pallas_call(
        _sb_paged_kernel,
        grid_spec=grid_spec,
        out_shape=jax.ShapeDtypeStruct((db, 1, SB_WIDTH), BF16),
        compiler_params=_params(("arbitrary", "arbitrary")),
        name="sb_paged",
    )(page_table, q_bf, k_new, v_new, sg, bias_col, uu, *([ck] * np_), *([cv] * np_))


def _pick_tile(t, cap):
    tm = min(t, cap)
    assert t % tm == 0
    return tm


def kernel(x_prompt, x_sample, cache_k, cache_v, page_table, state_conv, state_rec, c_prompt, c_sample,
           w_mod, b_mod, w_in, sb_bias, conv_w, a_log, dt_bias, gdn_norm_w, w_out, ln_g, ln_b):
    b, s, _ = x_prompt.shape
    db, t_new, _ = x_sample.shape
    assert t_new == 1 and s % GDN_TS == 0 and s % SB_TQ == 0
    assert page_table.shape[1] % PAGES_PER_STEP == 0 and cache_k.shape[2] == SB_TK

    mod = _mod_call(jnp.concatenate([c_sample, c_prompt], axis=0), w_mod, b_mod)
    mod_rows = mod.reshape(DEPTH, 3, db + b, 1, D_MODEL)

    uu = _cumsum_matrix()
    ub = _chunk_cumsum_matrix(GDN_TS)
    lb = ub.T
    gate_pad = ((0, 0), (GDN_HEADS, LANES - 2 * GDN_HEADS))

    hp, hs = x_prompt, x_sample.reshape(1, db, D_MODEL)
    cp_l, rp_l, ks_l, vs_l, cs_l, rs_l = [], [], [], [], [], []
    kv_prompt = ()
    for l in range(DEPTH):
        w_bf = jnp.pad(w_in[l], ((0, 0), (0, W_IN_PAD - w_in.shape[2]))).astype(BF16)
        wt_bf = jnp.concatenate(
            [w_in[l][:, COL_K:COL_G], w_in[l][:, COL_BA:COL_BA + SUBLANES]], axis=1).T.astype(BF16)
        w_top = w_out[l][:SB_WIDTH].astype(BF16)
        w_bot = w_out[l][SB_WIDTH:].astype(BF16)
        hrow = jnp.pad(jnp.stack([a_log[l], dt_bias[l]]), gate_pad)
        hcol = hrow[:, :SUBLANES].T
        nw = gdn_norm_w[l].reshape(1, GDN_HEAD_DIM)
        lng = ln_g[l].reshape(1, D_MODEL)
        lnb = ln_b[l].reshape(1, D_MODEL)

        shift, scale, gate = (mod_rows[l, i, db:] for i in range(3))
        tm = _pick_tile(s, 256)
        q_bf, kt, vt, kt_bf, vt_bf, sg, qkv, gz, gba, gbat = _inproj_call(
            hp, shift, scale, w_bf, wt_bf, tm, per_row_mod=False, transposed_kv=True,
            layer=l, kv_buffers=kv_prompt)
        kv_prompt = (kt, vt)
        y_sb = _sb_prompt_call(q_bf, kt_bf, vt_bf, sg, sb_bias[l], uu)
        y_gdn, conv_p, rec_p = _gdn_prompt_call(qkv, gz, gba, gbat, conv_w[l], hrow, hcol, nw, ub, lb)
        hp = _outproj_call(y_sb, y_gdn, hp, gate, w_top, w_bot, lng, lnb, _pick_tile(s, 512), False)
        cp_l.append(conv_p)
        rp_l.append(rec_p)

        shift, scale, gate = (mod[l, i, :db].reshape(1, db, D_MODEL) for i in range(3))
        q_bf, k, v, sg, qkv, gz, gba = _inproj_call(
            hs, shift, scale, w_bf, wt_bf, db, per_row_mod=True, transposed_kv=False)
        as_rows = lambda a: a.reshape(db, 1, a.shape[-1])
        y_sb = _sb_paged_call(l, page_table, as_rows(q_bf), as_rows(k), as_rows(v), as_rows(sg),
                              sb_bias[l].reshape(SB_HEADS, 1), uu, cache_k, cache_v)
        y_gdn, conv_s, rec_s = _gdn_step_call(
            l, as_rows(qkv), state_conv[l].reshape(db, 1, (CONV_W - 1) * GDN_CONV_DIM),
            as_rows(gz), as_rows(gba), state_rec, conv_w[l], hrow, nw)
        hs = _outproj_call(y_sb.reshape(1, db, SB_WIDTH), y_gdn.reshape(1, db, GDN_WIDTH), hs, gate,
                           w_top, w_bot, lng, lnb, db, True)
        ks_l.append(k.reshape(db, 1, SB_HEADS, SB_HEAD_DIM))
        vs_l.append(v.reshape(db, 1, SB_HEADS, SB_HEAD_DIM))
        cs_l.append(conv_s.reshape(db, CONV_W - 1, GDN_CONV_DIM))
        rs_l.append(rec_s)

    heads_last = lambda a: jnp.transpose(a.reshape(DEPTH, b, SB_HEADS, SB_HEAD_DIM, s), (0, 1, 4, 2, 3))
    k_prompt, v_prompt = (heads_last(a) for a in kv_prompt)
    return (hp, hs.reshape(db, 1, D_MODEL), k_prompt, v_prompt, jnp.stack(cp_l),
            jnp.stack(rp_l), jnp.stack(ks_l), jnp.stack(vs_l), jnp.stack(cs_l), jnp.stack(rs_l))
```

```python
import functools

import jax
import jax.numpy as jnp
from jax import lax
from jax.experimental import pallas as pl
from jax.experimental.pallas import tpu as pltpu

F32 = jnp.float32
BF16 = jnp.bfloat16

D_MODEL = 1024
DEPTH = 2
SB_HEADS = 8
SB_HEAD_DIM = 64
SB_WIDTH = SB_HEADS * SB_HEAD_DIM
SB_SCALE = SB_HEAD_DIM ** -0.5
GDN_HEADS = 4
GDN_HEAD_DIM = 128
GDN_WIDTH = GDN_HEADS * GDN_HEAD_DIM
GDN_CONV_DIM = 3 * GDN_WIDTH
CONV_W = 4
DN_ALPHA = (2 * DEPTH) ** 0.25
LN_EPS = 1e-5
RMS_EPS = 1e-6
L2_EPS = 1e-6

LANES = 128
SUBLANES = 8
COL_Q, COL_K, COL_V, COL_G = 0, SB_WIDTH, 2 * SB_WIDTH, 3 * SB_WIDTH
COL_QKV = 4 * SB_WIDTH
COL_Z = COL_QKV + GDN_CONV_DIM
COL_BA = COL_Z + GDN_WIDTH
W_IN_PAD = COL_BA + LANES
VMEM_LIMIT = 56 * 1024 * 1024

_NT = (((1,), (1,)), ((), ()))
_TN = (((0,), (0,)), ((), ()))


def _silu(x):
    return x * jax.nn.sigmoid(x)


def _softplus(x):
    return jnp.maximum(x, 0.0) + jnp.log1p(jnp.exp(-jnp.abs(x)))


def _split_bf16(x):
    hi = x.astype(BF16)
    lo = (x - hi.astype(F32)).astype(BF16)
    return hi, lo


def _params(sem):
    return pltpu.CompilerParams(dimension_semantics=sem, vmem_limit_bytes=VMEM_LIMIT)


def _mod_kernel(c_ref, w_ref, b_ref, o_ref):
    a = _silu(c_ref[...]).astype(BF16)
    o_ref[...] = jnp.dot(a, w_ref[...].astype(BF16), preferred_element_type=F32) + b_ref[...]


def _mod_call(c_all, w_mod, b_mod):
    n = c_all.shape[0]
    b4 = b_mod.reshape(DEPTH, 3, 1, D_MODEL)
    return pl.pallas_call(
        _mod_kernel,
        grid=(DEPTH, 3),
        in_specs=[
            pl.BlockSpec((n, D_MODEL), lambda l, p: (0, 0)),
            pl.BlockSpec((None, D_MODEL, D_MODEL), lambda l, p: (l, 0, p)),
            pl.BlockSpec((None, None, 1, D_MODEL), lambda l, p: (l, p, 0, 0)),
        ],
        out_specs=pl.BlockSpec((None, None, n, D_MODEL), lambda l, p: (l, p, 0, 0)),
        out_shape=jax.ShapeDtypeStruct((DEPTH, 3, n, D_MODEL), F32),
        compiler_params=_params(("arbitrary", "arbitrary")),
        name="adaln_mod",
    )(c_all, w_mod, b4)


SB_TQ = 1024
SB_UNROLL = 8
SB_TK = 128
LOG2E = 1.4426950408889634


def _l2norm(x):
    return x * lax.rsqrt(jnp.sum(x * x, axis=-1, keepdims=True) + L2_EPS)


def _inproj_kernel(transposed_kv, n_aliased, x_ref, shift_ref, scale_ref, w_ref, wt_ref, *refs):
    refs = refs[n_aliased:]
    u = (x_ref[...] * (1.0 + scale_ref[...]) + shift_ref[...]).astype(BF16)

    def seg(lo, width):
        return jnp.dot(u, w_ref[:, lo:lo + width], preferred_element_type=F32)

    if transposed_kv:
        (cw_ref, q_ref, kt_ref, vt_ref, ktb_ref, vtb_ref, sg_ref, qkv_ref, gz_ref, gba_ref, gbat_ref,
         conv_ref, xp_ref) = refs
        t = lax.dot_general(wt_ref[...], u, _NT, preferred_element_type=F32)
        kt = t[:SB_WIDTH]
        vt = t[SB_WIDTH:2 * SB_WIDTH]
        kt_ref[...] = kt
        vt_ref[...] = vt
        for j in range(u.shape[0] // SB_TK):
            ktb_ref[j] = kt[:, j * SB_TK:(j + 1) * SB_TK].astype(BF16)
            vtb_ref[j] = vt[:, j * SB_TK:(j + 1) * SB_TK].astype(BF16)
        gbat_ref[...] = t[2 * SB_WIDTH:]
    else:
        q_ref, k_ref, v_ref, sg_ref, qkv_ref, gz_ref, gba_ref = refs
        k_ref[...] = seg(COL_K, SB_WIDTH)
        v_ref[...] = seg(COL_V, SB_WIDTH)
    q_ref[...] = (seg(COL_Q, SB_WIDTH) * (SB_SCALE * LOG2E)).astype(BF16)
    sg_ref[...] = seg(COL_G, SB_WIDTH)
    gz_ref[...] = seg(COL_Z, GDN_WIDTH)
    gba_ref[...] = seg(COL_BA, LANES)
    if not transposed_kv:
        for j in range(3):
            qkv_ref[:, j * GDN_WIDTH:(j + 1) * GDN_WIDTH] = seg(COL_QKV + j * GDN_WIDTH, GDN_WIDTH)
        return

    ti = pl.program_id(1)
    tm = u.shape[0]
    pad = SUBLANES

    @pl.when(ti == 0)
    def _():
        xp_ref[0:pad, :] = jnp.zeros((pad, GDN_CONV_DIM), F32)

    for j in range(3):
        xp_ref[pad:pad + tm, j * GDN_WIDTH:(j + 1) * GDN_WIDTH] = seg(COL_QKV + j * GDN_WIDTH, GDN_WIDTH)
    y = xp_ref[pad:pad + tm, :] * cw_ref[CONV_W - 1:CONV_W, :]
    for i in range(CONV_W - 1):
        sh = CONV_W - 1 - i
        y = y + xp_ref[pad - sh:pad - sh + tm, :] * cw_ref[i:i + 1, :]
    qkv = _silu(y)
    tail = xp_ref[tm:tm + pad, :]
    xp_ref[0:pad, :] = tail
    for h in range(GDN_HEADS):
        lo = h * GDN_HEAD_DIM
        hi = lo + GDN_HEAD_DIM
        qkv_ref[:, lo:hi] = _l2norm(qkv[:, lo:hi]) * GDN_HEAD_DIM ** -0.5
        qkv_ref[:, GDN_WIDTH + lo:GDN_WIDTH + hi] = _l2norm(qkv[:, GDN_WIDTH + lo:GDN_WIDTH + hi])
    qkv_ref[:, 2 * GDN_WIDTH:] = qkv[:, 2 * GDN_WIDTH:]

    @pl.when(ti == pl.num_programs(1) - 1)
    def _():
        conv_ref[...] = tail[pad - (CONV_W - 1):pad, :]


def _inproj_call(x, shift, scale, w_bf, wt_bf, tm, per_row_mod, transposed_kv, layer=0, kv_buffers=(),
                 conv_w=None):
    g, t, _ = x.shape
    nt = t // tm
    if per_row_mod:
        mod_spec = pl.BlockSpec((None, tm, D_MODEL), lambda b, i: (b, i, 0))
    else:
        mod_spec = pl.BlockSpec((None, 1, D_MODEL), lambda b, i: (b, 0, 0))

    def tok(width, dtype):
        return (pl.BlockSpec((None, tm, width), lambda b, i: (b, i, 0)),
                jax.ShapeDtypeStruct((g, t, width), dtype))

    def chan(rows):
        return (pl.BlockSpec((None, rows, tm), lambda b, i: (b, 0, i)),
                jax.ShapeDtypeStruct((g, rows, t), F32))

    def blocks():
        return (pl.BlockSpec((None, tm // SB_TK, SB_WIDTH, SB_TK), lambda b, i: (b, i, 0, 0)),
                jax.ShapeDtypeStruct((g, t // SB_TK, SB_WIDTH, SB_TK), BF16))

    def layered():
        return (pl.BlockSpec((None, None, SB_WIDTH, tm), lambda b, i: (layer, b, 0, i)),
                jax.ShapeDtypeStruct((DEPTH, g, SB_WIDTH, t), F32))

    tail = [tok(SB_WIDTH, F32), tok(GDN_CONV_DIM, F32), tok(GDN_WIDTH, F32), tok(LANES, F32)]
    if transposed_kv:
        outs = [tok(SB_WIDTH, BF16), layered(), layered(), blocks(), blocks()] + tail
        outs.append(chan(SUBLANES))
        outs.append((pl.BlockSpec((None, CONV_W - 1, GDN_CONV_DIM), lambda b, i: (b, 0, 0)),
                     jax.ShapeDtypeStruct((g, CONV_W - 1, GDN_CONV_DIM), F32)))
        extra_in = [conv_w]
        extra_specs = [pl.BlockSpec((CONV_W, GDN_CONV_DIM), lambda b, i: (0, 0))]
        scratch = [pltpu.VMEM((tm + SUBLANES, GDN_CONV_DIM), F32)]
    else:
        outs = [tok(SB_WIDTH, BF16), tok(SB_WIDTH, F32), tok(SB_WIDTH, F32)] + tail
        extra_in, extra_specs, scratch = [], [], []
    n_in = 5
    return pl.pallas_call(
        functools.partial(_inproj_kernel, transposed_kv, len(kv_buffers)),
        grid=(g, nt),
        in_specs=[
            pl.BlockSpec((None, tm, D_MODEL), lambda b, i: (b, i, 0)),
            mod_spec, mod_spec,
            pl.BlockSpec((D_MODEL, W_IN_PAD), lambda b, i: (0, 0)),
            pl.BlockSpec(wt_bf.shape, lambda b, i: (0, 0)),
        ] + [pl.BlockSpec(memory_space=pl.ANY)] * len(kv_buffers) + extra_specs,
        out_specs=[o[0] for o in outs],
        out_shape=[o[1] for o in outs],
        scratch_shapes=scratch,
        input_output_aliases={n_in + j: 1 + j for j in range(len(kv_buffers))},
        compiler_params=_params(("arbitrary", "arbitrary")),
        name="in_proj",
    )(x, shift, scale, w_bf, wt_bf, *kv_buffers, *extra_in)


def _cumsum_matrix():
    j = lax.broadcasted_iota(jnp.int32, (2 * SB_TK, 2 * SB_TK), 0) % SB_TK
    s = lax.broadcasted_iota(jnp.int32, (2 * SB_TK, 2 * SB_TK), 1)
    return jnp.where((s >= SB_TK) | (j >= s), 1.0, 0.0).astype(BF16)


def _neg_abs(x):
    bits = lax.bitcast_convert_type(x, jnp.uint32) | jnp.uint32(0x80000000)
    return lax.bitcast_convert_type(bits, F32)


def _softplus2(z):
    return jnp.maximum(z, 0.0) + jnp.log(1.0 + jnp.exp2(_neg_abs(z))) * LOG2E


def _mask_leading(x, allowed):
    if allowed is None:
        return x
    n = allowed.shape[0]
    head = jnp.where(allowed, x[:n], 0.0)
    return head if x.shape[0] == n else jnp.concatenate([head, x[n:]], axis=0)


def _sb_scores(qh, kt_blk, bias2, uu, allowed):
    z = jnp.dot(qh, kt_blk, preferred_element_type=F32) + bias2
    hi, lo = _split_bf16(_mask_leading(_softplus2(z), allowed))
    return z, jnp.dot(jnp.concatenate([hi, lo], axis=1), uu, preferred_element_type=F32)


def _sb_apply(z, cs, carry, vt_blk, allowed):
    tk = vt_blk.shape[1]
    w = _mask_leading(jnp.exp2(z - carry - cs[:, :tk]), allowed)
    pv = lax.dot_general(w.astype(BF16), vt_blk, _NT, preferred_element_type=F32)
    return carry + cs[:, tk:], pv


def _sb_prompt_kernel(bias_ref, q_ref, k_ref, v_ref, sg_ref, uu_ref, y_ref, qm_ref, carry_ref, acc_ref):
    pair = pl.program_id(1)
    qi = pl.program_id(2)
    nsub = SB_TQ // SB_TK
    q = q_ref[...]
    uu = uu_ref[...]
    lane = lax.broadcasted_iota(jnp.int32, (SB_TQ, LANES), 1)
    for h in range(2):
        qm_ref[h] = q * jnp.where(lane // SB_HEAD_DIM == h, 1.0, 0.0).astype(BF16)
    carry_ref[...] = jnp.zeros_like(carry_ref)
    acc_ref[...] = jnp.zeros_like(acc_ref)
    bias2 = [bias_ref[2 * pair + h] * LOG2E for h in range(2)]

    row = lax.broadcasted_iota(jnp.int32, (SB_TK, SB_TK), 0)
    col = lax.broadcasted_iota(jnp.int32, (SB_TK, SB_TK), 1)
    causal = col < row

    def sweep(tasks):
        masks = [causal if on_diagonal else None for _, _, on_diagonal in tasks]
        scores = [[_sb_scores(qm_ref[h, first:, :], k_ref[kb], bias2[h], uu, allowed) for h in range(2)]
                  for (kb, first, _), allowed in zip(tasks, masks)]
        for (kb, first, _), allowed, per_head in zip(tasks, masks, scores):
            vt = v_ref[kb]
            for h, (z, cs) in enumerate(per_head):
                carry, pv = _sb_apply(z, cs, carry_ref[h, first:, :], vt, allowed)
                carry_ref[h, first:, :] = carry
                acc_ref[h, first:, :] += pv

    for j0 in reversed(range(0, nsub, SB_UNROLL)):
        sweep([(qi * nsub + j, j * SB_TK, True) for j in reversed(range(j0, j0 + SB_UNROLL))])

    def body(i, _):
        kb = qi * nsub - 1 - SB_UNROLL * i
        sweep([(kb - u, 0, False) for u in range(SB_UNROLL)])
        return 0

    lax.fori_loop(0, qi * (nsub // SB_UNROLL), body, 0)
    o = jnp.where(lane < SB_HEAD_DIM, acc_ref[0], acc_ref[1])
    y_ref[...] = (o * _silu(sg_ref[...])).astype(BF16)


def _sb_prompt_call(q_bf, kt_bf, vt_bf, sg, bias, uu):
    b, s, _ = q_bf.shape
    nq = s // SB_TQ
    npair = SB_WIDTH // LANES
    qspec = pl.BlockSpec((None, SB_TQ, LANES), lambda bi, p, i: (bi, i, p))
    kvspec = pl.BlockSpec((None, s // SB_TK, LANES, SB_TK), lambda bi, p, i: (bi, 0, p, 0))
    return pl.pallas_call(
        _sb_prompt_kernel,
        grid=(b, npair, nq),
        in_specs=[pl.BlockSpec(memory_space=pltpu.SMEM), qspec, kvspec, kvspec, qspec,
                  pl.BlockSpec((2 * SB_TK, 2 * SB_TK), lambda bi, p, i: (0, 0))],
        out_specs=qspec,
        out_shape=jax.ShapeDtypeStruct((b, s, SB_WIDTH), BF16),
        scratch_shapes=[pltpu.VMEM((2, SB_TQ, LANES), BF16),
                        pltpu.VMEM((2, SB_TQ, LANES), F32),
                        pltpu.VMEM((2, SB_TQ, LANES), F32)],
        compiler_params=_params(("arbitrary", "arbitrary", "arbitrary")),
        name="sb_prompt",
    )(bias, q_bf, kt_bf, vt_bf, sg, uu)


GDN_TS = 512
GDN_KCHUNK = 128
GDN_NCH = GDN_TS // GDN_KCHUNK
GDN_INV_BASE = 8


def _chunk_cumsum_matrix(ts):
    j = lax.broadcasted_iota(jnp.int32, (ts, ts), 0)
    s = lax.broadcasted_iota(jnp.int32, (ts, ts), 1)
    return jnp.where((j <= s) & (j // GDN_KCHUNK == s // GDN_KCHUNK), 1.0, 0.0).astype(BF16)


def _gdn_out(o, gz, norm_w):
    y = o * lax.rsqrt(jnp.mean(o * o, axis=-1, keepdims=True) + RMS_EPS) * norm_w
    return y * _silu(gz)


def _dot(a, b):
    return jnp.dot(a.astype(BF16), b.astype(BF16), preferred_element_type=F32)


def _dot_nt(a, b):
    return lax.dot_general(a.astype(BF16), b.astype(BF16), _NT, preferred_element_type=F32)


def _gdn_prompt_kernel(qkv_ref, gz_ref, gba_ref, gbat_ref, hrow_ref, hcol_ref, nw_ref,
                       ub_ref, lb_ref, y_ref, rec_ref, st_ref):
    si = pl.program_id(1)
    c = GDN_KCHUNK
    dk = GDN_HEAD_DIM

    @pl.when(si == 0)
    def _():
        st_ref[...] = jnp.zeros_like(st_ref)

    qkv = qkv_ref[...]

    gba = gba_ref[...]
    beta_c = jax.nn.sigmoid(gba)
    g_c = -jnp.exp(hrow_ref[0:1, :]) * _softplus(gba + hrow_ref[1:2, :])
    gbat = gbat_ref[...]
    g_r = -jnp.exp(hcol_ref[:, 0:1]) * _softplus(gbat + hcol_ref[:, 1:2])
    ub = ub_ref[...]
    g_hi, g_lo = _split_bf16(g_c)
    lb = lb_ref[...]
    gc_c = (jnp.dot(lb, g_hi, preferred_element_type=F32)
            + jnp.dot(lb, g_lo, preferred_element_type=F32))
    r_hi, r_lo = _split_bf16(g_r)
    gc_r = (jnp.dot(r_hi, ub, preferred_element_type=F32)
            + jnp.dot(r_lo, ub, preferred_element_type=F32))

    ri = lax.broadcasted_iota(jnp.int32, (c, c), 0)
    ci = lax.broadcasted_iota(jnp.int32, (c, c), 1)
    causal = ci <= ri
    strict = ci < ri
    nw = nw_ref[...]
    gz = gz_ref[...]

    items = [(ch, h) for ch in range(GDN_NCH) for h in range(GDN_HEADS)]
    qn, kn, vv = {}, {}, {}
    for h in range(GDN_HEADS):
        lo = h * dk
        qn[h] = qkv[:, lo:lo + dk]
        kn[h] = qkv[:, GDN_WIDTH + lo:GDN_WIDTH + lo + dk]
        vv[h] = qkv[:, 2 * GDN_WIDTH + lo:2 * GDN_WIDTH + lo + dk]
    rows = lambda ch: slice(ch * c, (ch + 1) * c)
    kk = {it: kn[it[1]][rows(it[0])] for it in items}
    qq = {it: qn[it[1]][rows(it[0])] for it in items}
    bcol = {it: beta_c[rows(it[0]), it[1]:it[1] + 1] for it in items}
    gcol = {it: gc_c[rows(it[0]), GDN_HEADS + it[1]:GDN_HEADS + it[1] + 1] for it in items}
    grow = {it: gc_r[GDN_HEADS + it[1]:GDN_HEADS + it[1] + 1, rows(it[0])] for it in items}
    decay = {it: jnp.where(causal, jnp.exp(jnp.where(causal, gcol[it] - grow[it], 0.0)), 0.0)
             for it in items}
    kb = {it: kk[it] * bcol[it] for it in items}
    kk_bf = {it: kk[it].astype(BF16) for it in items}
    x = {it: -jnp.where(strict, _dot_nt(kb[it], kk_bf[it]) * decay[it], 0.0) for it in items}
    blk = lambda w: (ri // w) == (ci // w)
    n = {it: jnp.where(blk(GDN_INV_BASE), x[it], 0.0) for it in items}
    p = dict(n)
    for _ in range(GDN_INV_BASE.bit_length() - 2):
        p = {it: _dot(p[it], p[it]) for it in items}
        n = {it: n[it] + p[it] + _dot(n[it], p[it]) for it in items}
    w = GDN_INV_BASE
    while w < c:
        merge = blk(2 * w) & jnp.logical_not(blk(w))
        off = {it: jnp.where(merge, x[it], 0.0) for it in items}
        yy = {it: off[it] + _dot(n[it], off[it]) for it in items}
        n = {it: n[it] + yy[it] + _dot(yy[it], n[it]) for it in items}
        w *= 2
    egc = {it: jnp.exp(gcol[it]) for it in items}
    rhs = {it: jnp.concatenate([vv[it[1]][rows(it[0])] * bcol[it], kb[it] * egc[it]], axis=1)
           for it in items}
    uw = {it: rhs[it] + _dot(n[it], rhs[it]) for it in items}
    intra = {it: jnp.where(causal, _dot_nt(qq[it], kk_bf[it]) * decay[it], 0.0).astype(BF16)
             for it in items}
    qg = {it: (qq[it] * egc[it]).astype(BF16) for it in items}
    glast = {it: grow[it][:, c - 1:c] for it in items}
    kgl = {it: (kk[it] * jnp.exp(glast[it] - gcol[it])).astype(BF16) for it in items}

    state = {h: st_ref[h] for h in range(GDN_HEADS)}
    for ch in range(GDN_NCH):
        its = [(ch, h) for h in range(GDN_HEADS)]
        sb = {it: state[it[1]].astype(BF16) for it in its}
        v_new = {it: (uw[it][:, :dk] - _dot(uw[it][:, dk:], sb[it])).astype(BF16) for it in its}
        o = {it: jnp.dot(qg[it], sb[it], preferred_element_type=F32)
             + jnp.dot(intra[it], v_new[it], preferred_element_type=F32) for it in its}
        for it in its:
            h = it[1]
            state[h] = (state[h] * jnp.exp(glast[it])
                        + lax.dot_general(kgl[it], v_new[it], _TN, preferred_element_type=F32))
            y_ref[rows(ch), h * dk:(h + 1) * dk] = _gdn_out(
                o[it], gz[rows(ch), h * dk:(h + 1) * dk], nw).astype(BF16)
    for h in range(GDN_HEADS):
        st_ref[h] = state[h]

    @pl.when(si == pl.num_programs(1) - 1)
    def _():
        rec_ref[...] = st_ref[...]


def _gdn_prompt_call(qkv, gz, gba, gbat, hrow, hcol, nw, ub, lb):
    b, s, _ = qkv.shape
    ns = s // GDN_TS
    const2 = lambda shape: pl.BlockSpec(shape, lambda bi, i: (0, 0))
    return pl.pallas_call(
        _gdn_prompt_kernel,
        grid=(b, ns),
        in_specs=[
            pl.BlockSpec((None, GDN_TS, GDN_CONV_DIM), lambda bi, i: (bi, i, 0)),
            pl.BlockSpec((None, GDN_TS, GDN_WIDTH), lambda bi, i: (bi, i, 0)),
            pl.BlockSpec((None, GDN_TS, LANES), lambda bi, i: (bi, i, 0)),
            pl.BlockSpec((None, SUBLANES, GDN_TS), lambda bi, i: (bi, 0, i)),
            const2((2, LANES)), const2((SUBLANES, 2)),
            const2((1, GDN_HEAD_DIM)), const2((GDN_TS, GDN_TS)), const2((GDN_TS, GDN_TS)),
        ],
        out_specs=[
            pl.BlockSpec((None, GDN_TS, GDN_WIDTH), lambda bi, i: (bi, i, 0)),
            pl.BlockSpec((None, GDN_HEADS, GDN_HEAD_DIM, GDN_HEAD_DIM), lambda bi, i: (bi, 0, 0, 0)),
        ],
        out_shape=[
            jax.ShapeDtypeStruct((b, s, GDN_WIDTH), BF16),
            jax.ShapeDtypeStruct((b, GDN_HEADS, GDN_HEAD_DIM, GDN_HEAD_DIM), F32),
        ],
        scratch_shapes=[pltpu.VMEM((GDN_HEADS, GDN_HEAD_DIM, GDN_HEAD_DIM), F32)],
        compiler_params=_params(("arbitrary", "arbitrary")),
        name="gdn_prompt",
    )(qkv, gz, gba, gbat, hrow, hcol, nw, ub, lb)


def _outproj_kernel(ysb_ref, ygdn_ref, x_ref, gate_ref, wt_ref, wb_ref, g_ref, b_ref, o_ref):
    mixed = (jnp.dot(ysb_ref[...], wt_ref[...], preferred_element_type=F32)
             + jnp.dot(ygdn_ref[...], wb_ref[...], preferred_element_type=F32))
    r = DN_ALPHA * x_ref[...] + gate_ref[...] * mixed
    mu = jnp.mean(r, axis=-1, keepdims=True)
    rc = r - mu
    var = jnp.mean(rc * rc, axis=-1, keepdims=True)
    o_ref[...] = rc * lax.rsqrt(var + LN_EPS) * g_ref[...] + b_ref[...]


def _outproj_call(y_sb, y_gdn, x, gate, w_top, w_bot, ln_g, ln_b, tm, per_row_mod):
    g, t, _ = x.shape
    nt = t // tm
    if per_row_mod:
        gate_spec = pl.BlockSpec((None, tm, D_MODEL), lambda b, i: (b, i, 0))
    else:
        gate_spec = pl.BlockSpec((None, 1, D_MODEL), lambda b, i: (b, 0, 0))
    half = pl.BlockSpec((None, tm, SB_WIDTH), lambda b, i: (b, i, 0))
    full = pl.BlockSpec((None, tm, D_MODEL), lambda b, i: (b, i, 0))
    const = lambda shape: pl.BlockSpec(shape, lambda b, i: (0, 0))
    return pl.pallas_call(
        _outproj_kernel,
        grid=(g, nt),
        in_specs=[half, half, full, gate_spec, const((SB_WIDTH, D_MODEL)), const((GDN_WIDTH, D_MODEL)),
                  const((1, D_MODEL)), const((1, D_MODEL))],
        out_specs=full,
        out_shape=jax.ShapeDtypeStruct((g, t, D_MODEL), F32),
        compiler_params=_params(("arbitrary", "arbitrary")),
        name="out_proj_ln",
    )(y_sb, y_gdn, x, gate, w_top, w_bot, ln_g, ln_b)


def _gdn_step_kernel(qkv_ref, cs_ref, gz_ref, gba_ref, rec_ref, cw_ref, hrow_ref, nw_ref,
                     y_ref, cs_out_ref, rec_out_ref):
    cdim = GDN_CONV_DIM
    x = qkv_ref[...]
    y = x * cw_ref[CONV_W - 1:CONV_W, :]
    for i in range(CONV_W - 1):
        y = y + cs_ref[:, i * cdim:(i + 1) * cdim] * cw_ref[i:i + 1, :]
    qkv = _silu(y)
    for i in range(CONV_W - 2):
        cs_out_ref[:, i * cdim:(i + 1) * cdim] = cs_ref[:, (i + 1) * cdim:(i + 2) * cdim]
    cs_out_ref[:, (CONV_W - 2) * cdim:(CONV_W - 1) * cdim] = x

    gba = gba_ref[...]
    beta = jax.nn.sigmoid(gba)
    g = -jnp.exp(hrow_ref[0:1, :]) * _softplus(gba + hrow_ref[1:2, :])
    eye = (lax.broadcasted_iota(jnp.int32, (GDN_HEAD_DIM, GDN_HEAD_DIM), 0)
           == lax.broadcasted_iota(jnp.int32, (GDN_HEAD_DIM, GDN_HEAD_DIM), 1))

    def column(row):
        return jnp.sum(jnp.where(eye, row, 0.0), axis=1, keepdims=True)

    nw = nw_ref[...]
    gz = gz_ref[...]
    for h in range(GDN_HEADS):
        lo = h * GDN_HEAD_DIM
        hi = lo + GDN_HEAD_DIM
        q_h = _l2norm(qkv[:, lo:hi]) * GDN_HEAD_DIM ** -0.5
        k_h = _l2norm(qkv[:, GDN_WIDTH + lo:GDN_WIDTH + hi])
        v_h = qkv[:, 2 * GDN_WIDTH + lo:2 * GDN_WIDTH + hi]
        s = rec_ref[h] * jnp.exp(g[:, GDN_HEADS + h:GDN_HEADS + h + 1])
        kcol = column(k_h)
        delta = (v_h - jnp.sum(s * kcol, axis=0, keepdims=True)) * beta[:, h:h + 1]
        s = s + kcol * delta
        rec_out_ref[h] = s
        o = jnp.sum(s * column(q_h), axis=0, keepdims=True)
        y_ref[:, lo:hi] = _gdn_out(o, gz[:, lo:hi], nw).astype(BF16)


def _gdn_step_call(layer, qkv, conv_state, gz, gba, state_rec, cw, hrow, nw):
    db = qkv.shape[0]
    row = lambda width: pl.BlockSpec((None, 1, width), lambda b: (b, 0, 0))
    const = lambda shape: pl.BlockSpec(shape, lambda b: (0, 0))
    rec_shape = (GDN_HEADS, GDN_HEAD_DIM, GDN_HEAD_DIM)
    cs_width = (CONV_W - 1) * GDN_CONV_DIM
    return pl.pallas_call(
        _gdn_step_kernel,
        grid=(db,),
        in_specs=[row(GDN_CONV_DIM), row(cs_width), row(GDN_WIDTH), row(LANES),
                  pl.BlockSpec((None, None) + rec_shape, lambda b: (layer, b, 0, 0, 0)),
                  const((CONV_W, GDN_CONV_DIM)), const((2, LANES)), const((1, GDN_HEAD_DIM))],
        out_specs=[row(GDN_WIDTH), row(cs_width),
                   pl.BlockSpec((None,) + rec_shape, lambda b: (b, 0, 0, 0))],
        out_shape=[jax.ShapeDtypeStruct((db, 1, GDN_WIDTH), BF16),
                   jax.ShapeDtypeStruct((db, 1, cs_width), F32),
                   jax.ShapeDtypeStruct((db,) + rec_shape, F32)],
        compiler_params=_params(("arbitrary",)),
        name="gdn_step",
    )(qkv, conv_state, gz, gba, state_rec, cw, hrow, nw)


PAGES_PER_STEP = 32


def _sb_paged_kernel(pt_ref, q_ref, kn_ref, vn_ref, sg_ref, bias_ref, uu_ref, *rest):
    del pt_ref
    np_ = PAGES_PER_STEP
    k_refs = rest[:np_]
    v_refs = rest[np_:2 * np_]
    y_ref, qm_ref, carry_ref, acc_ref = rest[2 * np_:]
    gi = pl.program_id(1)
    head = lax.broadcasted_iota(jnp.int32, (SB_HEADS, SB_WIDTH), 0)
    chan = lax.broadcasted_iota(jnp.int32, (SB_HEADS, SB_WIDTH), 1)
    own = chan // SB_HEAD_DIM == head

    @pl.when(gi == 0)
    def _():
        qm_ref[...] = jnp.where(own, q_ref[...].astype(F32), 0.0)
        carry_ref[...] = jnp.zeros_like(carry_ref)
        acc_ref[...] = jnp.zeros_like(acc_ref)

    qm32 = qm_ref[...]
    qm = qm32.astype(BF16)
    bias = bias_ref[...] * LOG2E
    tk = SB_TK
    kt_all = jnp.concatenate([r[...].astype(BF16) for r in k_refs], axis=1)
    z = jnp.dot(qm, kt_all, preferred_element_type=F32) + bias
    sp = _softplus2(z)
    hi, lo = _split_bf16(jnp.concatenate([sp[:, i * tk:(i + 1) * tk] for i in range(np_)], axis=0))
    cs = jnp.dot(jnp.concatenate([hi, lo], axis=1), uu_ref[...], preferred_element_type=F32)
    carry = carry_ref[...]
    ws = [None] * np_
    for i in reversed(range(np_)):
        cs_i = cs[i * SB_HEADS:(i + 1) * SB_HEADS]
        ws[i] = jnp.exp2(z[:, i * tk:(i + 1) * tk] - carry - cs_i[:, :tk])
        carry = carry + cs_i[:, tk:]
    carry_ref[...] = carry
    vt_all = jnp.concatenate([r[...].astype(BF16) for r in v_refs], axis=1)
    acc = acc_ref[...] + lax.dot_general(jnp.concatenate(ws, axis=1).astype(BF16), vt_all, _NT,
                                         preferred_element_type=F32)
    acc_ref[...] = acc

    @pl.when(gi == pl.num_programs(1) - 1)
    def _():
        t_new = kn_ref.shape[0]
        z_new = jnp.sum(qm32 * kn_ref[...], axis=-1, keepdims=True) + bias
        q_idx = lax.broadcasted_iota(jnp.int32, (SB_HEADS, t_new), 1) + (t_new - 1)
        k_idx = lax.broadcasted_iota(jnp.int32, (SB_HEADS, t_new), 1)
        w_new = jnp.where(k_idx < q_idx, jnp.exp2(z_new - _softplus(z_new / LOG2E) * LOG2E), 0.0)
        total = acc + w_new * vn_ref[...]
        o = jnp.sum(jnp.where(own, total, 0.0), axis=0, keepdims=True)
        y_ref[...] = (o * _silu(sg_ref[...])).astype(BF16)


def _sb_paged_call(layer, page_table, q_bf, k_new, v_new, sg, bias_col, uu, cache_k, cache_v):
    db, n_pages = page_table.shape
    np_ = PAGES_PER_STEP
    ng = n_pages // np_
    page = cache_k.shape[2]
    ck = jnp.transpose(cache_k, (0, 1, 3, 4, 2)).reshape(cache_k.shape[0], cache_k.shape[1], SB_WIDTH, page)
    cv = jnp.transpose(cache_v, (0, 1, 3, 4, 2)).reshape(cache_v.shape[0], cache_v.shape[1], SB_WIDTH, page)
    row = pl.BlockSpec((None, 1, SB_WIDTH), lambda b, g, pt: (b, 0, 0))

    def page_spec(i):
        return pl.BlockSpec(
            (None, None, SB_WIDTH, page),
            lambda b, g, pt: (layer, pt[b, (ng - 1 - g) * np_ + i], 0, 0))

    grid_spec = pltpu.PrefetchScalarGridSpec(
        num_scalar_prefetch=1,
        grid=(db, ng),
        in_specs=[row, row, row, row,
                  pl.BlockSpec((SB_HEADS, 1), lambda b, g, pt: (0, 0)),
                  pl.BlockSpec((2 * SB_TK, 2 * SB_TK), lambda b, g, pt: (0, 0))]
                 + [page_spec(i) for i in range(np_)] * 2,
        out_specs=row,
        scratch_shapes=[pltpu.VMEM((SB_HEADS, SB_WIDTH), F32),
                        pltpu.VMEM((SB_HEADS, SB_TK), F32),
                        pltpu.VMEM((SB_HEADS, SB_WIDTH), F32)],
    )
    return pl.pallas_call(
        _sb_paged_kernel,
        grid_spec=grid_spec,
        out_shape=jax.ShapeDtypeStruct((db, 1, SB_WIDTH), BF16),
        compiler_params=_params(("arbitrary", "arbitrary")),
        name="sb_paged",
    )(page_table, q_bf, k_new, v_new, sg, bias_col, uu, *([ck] * np_), *([cv] * np_))


def _pick_tile(t, cap):
    tm = min(t, cap)
    assert t % tm == 0
    return tm


def kernel(x_prompt, x_sample, cache_k, cache_v, page_table, state_conv, state_rec, c_prompt, c_sample,
           w_mod, b_mod, w_in, sb_bias, conv_w, a_log, dt_bias, gdn_norm_w, w_out, ln_g, ln_b):
    b, s, _ = x_prompt.shape
    db, t_new, _ = x_sample.shape
    assert t_new == 1 and s % GDN_TS == 0 and s % SB_TQ == 0
    assert page_table.shape[1] % PAGES_PER_STEP == 0 and cache_k.shape[2] == SB_TK

    mod = _mod_call(jnp.concatenate([c_sample, c_prompt], axis=0), w_mod, b_mod)
    mod_rows = mod.reshape(DEPTH, 3, db + b, 1, D_MODEL)

    uu = _cumsum_matrix()
    ub = _chunk_cumsum_matrix(GDN_TS)
    lb = ub.T
    gate_pad = ((0, 0), (GDN_HEADS, LANES - 2 * GDN_HEADS))

    hp, hs = x_prompt, x_sample.reshape(1, db, D_MODEL)
    cp_l, rp_l, ks_l, vs_l, cs_l, rs_l = [], [], [], [], [], []
    kv_prompt = ()
    for l in range(DEPTH):
        w_bf = jnp.pad(w_in[l], ((0, 0), (0, W_IN_PAD - w_in.shape[2]))).astype(BF16)
        wt_bf = jnp.concatenate(
            [w_in[l][:, COL_K:COL_G], w_in[l][:, COL_BA:COL_BA + SUBLANES]], axis=1).T.astype(BF16)
        w_top = w_out[l][:SB_WIDTH].astype(BF16)
        w_bot = w_out[l][SB_WIDTH:].astype(BF16)
        hrow = jnp.pad(jnp.stack([a_log[l], dt_bias[l]]), gate_pad)
        hcol = hrow[:, :SUBLANES].T
        nw = gdn_norm_w[l].reshape(1, GDN_HEAD_DIM)
        lng = ln_g[l].reshape(1, D_MODEL)
        lnb = ln_b[l].reshape(1, D_MODEL)

        shift, scale, gate = (mod_rows[l, i, db:] for i in range(3))
        tm = _pick_tile(s, 256)
        q_bf, kt, vt, kt_bf, vt_bf, sg, qkv, gz, gba, gbat, conv_p = _inproj_call(
            hp, shift, scale, w_bf, wt_bf, tm, per_row_mod=False, transposed_kv=True,
            layer=l, kv_buffers=kv_prompt, conv_w=conv_w[l])
        kv_prompt = (kt, vt)
        y_sb = _sb_prompt_call(q_bf, kt_bf, vt_bf, sg, sb_bias[l], uu)
        y_gdn, rec_p = _gdn_prompt_call(qkv, gz, gba, gbat, hrow, hcol, nw, ub, lb)
        hp = _outproj_call(y_sb, y_gdn, hp, gate, w_top, w_bot, lng, lnb, _pick_tile(s, 512), False)
        cp_l.append(conv_p)
        rp_l.append(rec_p)

        shift, scale, gate = (mod[l, i, :db].reshape(1, db, D_MODEL) for i in range(3))
        q_bf, k, v, sg, qkv, gz, gba = _inproj_call(
            hs, shift, scale, w_bf, wt_bf, db, per_row_mod=True, transposed_kv=False)
        as_rows = lambda a: a.reshape(db, 1, a.shape[-1])
        y_sb = _sb_paged_call(l, page_table, as_rows(q_bf), as_rows(k), as_rows(v), as_rows(sg),
                              sb_bias[l].reshape(SB_HEADS, 1), uu, cache_k, cache_v)
        y_gdn, conv_s, rec_s = _gdn_step_call(
            l, as_rows(qkv), state_conv[l].reshape(db, 1, (CONV_W - 1) * GDN_CONV_DIM),
            as_rows(gz), as_rows(gba), state_rec, conv_w[l], hrow, nw)
        hs = _outproj_call(y_sb.reshape(1, db, SB_WIDTH), y_gdn.reshape(1, db, GDN_WIDTH), hs, gate,
                           w_top, w_bot, lng, lnb, db, True)
        ks_l.append(k.reshape(db, 1, SB_HEADS, SB_HEAD_DIM))
        vs_l.append(v.reshape(db, 1, SB_HEADS, SB_HEAD_DIM))
        cs_l.append(conv_s.reshape(db, CONV_W - 1, GDN_CONV_DIM))
        rs_l.append(rec_s)

    heads_last = lambda a: jnp.transpose(a.reshape(DEPTH, b, SB_HEADS, SB_HEAD_DIM, s), (0, 1, 4, 2, 3))
    k_prompt, v_prompt = (heads_last(a) for a in kv_prompt)
    return (hp, hs.reshape(db, 1, D_MODEL), k_prompt, v_prompt, jnp.stack(cp_l),
            jnp.stack(rp_l), jnp.stack(ks_l), jnp.stack(vs_l), jnp.stack(cs_l), jnp.stack(rs_l))
```

```python
import functools

import jax
import jax.numpy as jnp
from jax import lax
from jax.experimental import pallas as pl
from jax.experimental.pallas import tpu as pltpu

F32 = jnp.float32
BF16 = jnp.bfloat16

D_MODEL = 1024
DEPTH = 2
SB_HEADS = 8
SB_HEAD_DIM = 64
SB_WIDTH = SB_HEADS * SB_HEAD_DIM
SB_SCALE = SB_HEAD_DIM ** -0.5
GDN_HEADS = 4
GDN_HEAD_DIM = 128
GDN_WIDTH = GDN_HEADS * GDN_HEAD_DIM
GDN_CONV_DIM = 3 * GDN_WIDTH
CONV_W = 4
DN_ALPHA = (2 * DEPTH) ** 0.25
LN_EPS = 1e-5
RMS_EPS = 1e-6
L2_EPS = 1e-6

LANES = 128
SUBLANES = 8
COL_Q, COL_K, COL_V, COL_G = 0, SB_WIDTH, 2 * SB_WIDTH, 3 * SB_WIDTH
COL_QKV = 4 * SB_WIDTH
COL_Z = COL_QKV + GDN_CONV_DIM
COL_BA = COL_Z + GDN_WIDTH
W_IN_PAD = COL_BA + LANES
VMEM_LIMIT = 56 * 1024 * 1024

_NT = (((1,), (1,)), ((), ()))
_TN = (((0,), (0,)), ((), ()))


def _silu(x):
    return x * jax.nn.sigmoid(x)


def _softplus(x):
    return jnp.maximum(x, 0.0) + jnp.log1p(jnp.exp(-jnp.abs(x)))


def _split_bf16(x):
    hi = x.astype(BF16)
    lo = (x - hi.astype(F32)).astype(BF16)
    return hi, lo


def _params(sem):
    return pltpu.CompilerParams(dimension_semantics=sem, vmem_limit_bytes=VMEM_LIMIT)


def _mod_kernel(c_ref, w_ref, b_ref, o_ref):
    a = _silu(c_ref[...]).astype(BF16)
    o_ref[...] = jnp.dot(a, w_ref[...].astype(BF16), preferred_element_type=F32) + b_ref[...]


def _mod_call(c_all, w_mod, b_mod):
    n = c_all.shape[0]
    b4 = b_mod.reshape(DEPTH, 3, 1, D_MODEL)
    return pl.pallas_call(
        _mod_kernel,
        grid=(DEPTH, 3),
        in_specs=[
            pl.BlockSpec((n, D_MODEL), lambda l, p: (0, 0)),
            pl.BlockSpec((None, D_MODEL, D_MODEL), lambda l, p: (l, 0, p)),
            pl.BlockSpec((None, None, 1, D_MODEL), lambda l, p: (l, p, 0, 0)),
        ],
        out_specs=pl.BlockSpec((None, None, n, D_MODEL), lambda l, p: (l, p, 0, 0)),
        out_shape=jax.ShapeDtypeStruct((DEPTH, 3, n, D_MODEL), F32),
        compiler_params=_params(("arbitrary", "arbitrary")),
        name="adaln_mod",
    )(c_all, w_mod, b4)


SB_TQ = 1024
SB_UNROLL = 8
SB_TK = 128
LOG2E = 1.4426950408889634


def _l2norm(x):
    return x * lax.rsqrt(jnp.sum(x * x, axis=-1, keepdims=True) + L2_EPS)


def _inproj_kernel(transposed_kv, n_aliased, x_ref, shift_ref, scale_ref, w_ref, wt_ref, *refs):
    refs = refs[n_aliased:]
    u = (x_ref[...] * (1.0 + scale_ref[...]) + shift_ref[...]).astype(BF16)

    def seg(lo, width):
        return jnp.dot(u, w_ref[:, lo:lo + width], preferred_element_type=F32)

    if transposed_kv:
        (cw_ref, q_ref, kt_ref, vt_ref, ktb_ref, vtb_ref, sg_ref, qkv_ref, gz_ref, gba_ref, gbat_ref,
         conv_ref, xp_ref) = refs
        t = lax.dot_general(wt_ref[...], u, _NT, preferred_element_type=F32)
        kt = t[:SB_WIDTH]
        vt = t[SB_WIDTH:2 * SB_WIDTH]
        kt_ref[...] = kt
        vt_ref[...] = vt
        for j in range(u.shape[0] // SB_TK):
            ktb_ref[j] = kt[:, j * SB_TK:(j + 1) * SB_TK].astype(BF16)
            vtb_ref[j] = vt[:, j * SB_TK:(j + 1) * SB_TK].astype(BF16)
        gbat_ref[...] = t[2 * SB_WIDTH:]
    else:
        q_ref, k_ref, v_ref, sg_ref, qkv_ref, gz_ref, gba_ref = refs
        k_ref[...] = seg(COL_K, SB_WIDTH)
        v_ref[...] = seg(COL_V, SB_WIDTH)
    q_ref[...] = (seg(COL_Q, SB_WIDTH) * (SB_SCALE * LOG2E)).astype(BF16)
    sg_ref[...] = seg(COL_G, SB_WIDTH)
    gz_ref[...] = seg(COL_Z, GDN_WIDTH)
    gba_ref[...] = seg(COL_BA, LANES)
    if not transposed_kv:
        for j in range(3):
            qkv_ref[:, j * GDN_WIDTH:(j + 1) * GDN_WIDTH] = seg(COL_QKV + j * GDN_WIDTH, GDN_WIDTH)
        return

    ti = pl.program_id(1)
    tm = u.shape[0]
    pad = SUBLANES

    @pl.when(ti == 0)
    def _():
        xp_ref[0:pad, :] = jnp.zeros((pad, GDN_CONV_DIM), F32)

    for j in range(3):
        xp_ref[pad:pad + tm, j * GDN_WIDTH:(j + 1) * GDN_WIDTH] = seg(COL_QKV + j * GDN_WIDTH, GDN_WIDTH)
    y = xp_ref[pad:pad + tm, :] * cw_ref[CONV_W - 1:CONV_W, :]
    for i in range(CONV_W - 1):
        sh = CONV_W - 1 - i
        y = y + xp_ref[pad - sh:pad - sh + tm, :] * cw_ref[i:i + 1, :]
    qkv = _silu(y)
    tail = xp_ref[tm:tm + pad, :]
    xp_ref[0:pad, :] = tail
    for h in range(GDN_HEADS):
        lo = h * GDN_HEAD_DIM
        hi = lo + GDN_HEAD_DIM
        qkv_ref[:, lo:hi] = _l2norm(qkv[:, lo:hi]) * GDN_HEAD_DIM ** -0.5
        qkv_ref[:, GDN_WIDTH + lo:GDN_WIDTH + hi] = _l2norm(qkv[:, GDN_WIDTH + lo:GDN_WIDTH + hi])
    qkv_ref[:, 2 * GDN_WIDTH:] = qkv[:, 2 * GDN_WIDTH:]

    @pl.when(ti == pl.num_programs(1) - 1)
    def _():
        conv_ref[...] = tail[pad - (CONV_W - 1):pad, :]


def _inproj_call(x, shift, scale, w_bf, wt_bf, tm, per_row_mod, transposed_kv, layer=0, kv_buffers=(),
                 conv_w=None):
    g, t, _ = x.shape
    nt = t // tm
    if per_row_mod:
        mod_spec = pl.BlockSpec((None, tm, D_MODEL), lambda b, i: (b, i, 0))
    else:
        mod_spec = pl.BlockSpec((None, 1, D_MODEL), lambda b, i: (b, 0, 0))

    def tok(width, dtype):
        return (pl.BlockSpec((None, tm, width), lambda b, i: (b, i, 0)),
                jax.ShapeDtypeStruct((g, t, width), dtype))

    def chan(rows):
        return (pl.BlockSpec((None, rows, tm), lambda b, i: (b, 0, i)),
                jax.ShapeDtypeStruct((g, rows, t), F32))

    def blocks():
        return (pl.BlockSpec((None, tm // SB_TK, SB_WIDTH, SB_TK), lambda b, i: (b, i, 0, 0)),
                jax.ShapeDtypeStruct((g, t // SB_TK, SB_WIDTH, SB_TK), BF16))

    def layered():
        return (pl.BlockSpec((None, None, SB_WIDTH, tm), lambda b, i: (layer, b, 0, i)),
                jax.ShapeDtypeStruct((DEPTH, g, SB_WIDTH, t), F32))

    tail = [tok(SB_WIDTH, F32), tok(GDN_CONV_DIM, F32), tok(GDN_WIDTH, F32), tok(LANES, F32)]
    if transposed_kv:
        outs = [tok(SB_WIDTH, BF16), layered(), layered(), blocks(), blocks()] + tail
        outs.append(chan(SUBLANES))
        outs.append((pl.BlockSpec((None, CONV_W - 1, GDN_CONV_DIM), lambda b, i: (b, 0, 0)),
                     jax.ShapeDtypeStruct((g, CONV_W - 1, GDN_CONV_DIM), F32)))
        extra_in = [conv_w]
        extra_specs = [pl.BlockSpec((CONV_W, GDN_CONV_DIM), lambda b, i: (0, 0))]
        scratch = [pltpu.VMEM((tm + SUBLANES, GDN_CONV_DIM), F32)]
    else:
        outs = [tok(SB_WIDTH, BF16), tok(SB_WIDTH, F32), tok(SB_WIDTH, F32)] + tail
        extra_in, extra_specs, scratch = [], [], []
    n_in = 5
    return pl.pallas_call(
        functools.partial(_inproj_kernel, transposed_kv, len(kv_buffers)),
        grid=(g, nt),
        in_specs=[
            pl.BlockSpec((None, tm, D_MODEL), lambda b, i: (b, i, 0)),
            mod_spec, mod_spec,
            pl.BlockSpec((D_MODEL, W_IN_PAD), lambda b, i: (0, 0)),
            pl.BlockSpec(wt_bf.shape, lambda b, i: (0, 0)),
        ] + [pl.BlockSpec(memory_space=pl.ANY)] * len(kv_buffers) + extra_specs,
        out_specs=[o[0] for o in outs],
        out_shape=[o[1] for o in outs],
        scratch_shapes=scratch,
        input_output_aliases={n_in + j: 1 + j for j in range(len(kv_buffers))},
        compiler_params=_params(("arbitrary", "arbitrary")),
        name="in_proj",
    )(x, shift, scale, w_bf, wt_bf, *kv_buffers, *extra_in)


def _cumsum_matrix():
    j = lax.broadcasted_iota(jnp.int32, (2 * SB_TK, 2 * SB_TK), 0) % SB_TK
    s = lax.broadcasted_iota(jnp.int32, (2 * SB_TK, 2 * SB_TK), 1)
    return jnp.where((s >= SB_TK) | (j >= s), 1.0, 0.0).astype(BF16)


def _neg_abs(x):
    bits = lax.bitcast_convert_type(x, jnp.uint32) | jnp.uint32(0x80000000)
    return lax.bitcast_convert_type(bits, F32)


def _softplus2(z):
    return jnp.maximum(z, 0.0) + jnp.log(1.0 + jnp.exp2(_neg_abs(z))) * LOG2E


def _mask_leading(x, allowed):
    if allowed is None:
        return x
    n = allowed.shape[0]
    head = jnp.where(allowed, x[:n], 0.0)
    return head if x.shape[0] == n else jnp.concatenate([head, x[n:]], axis=0)


def _sb_scores(qh, kt_blk, bias2, u_incl, allowed):
    z = jnp.dot(qh, kt_blk, preferred_element_type=F32) + bias2
    rest = _mask_leading(_softplus2(z), allowed).astype(BF16)
    return z, jnp.dot(rest, u_incl, preferred_element_type=F32)


def _sb_apply(z, cs, carry, vt_blk, allowed):
    w = _mask_leading(jnp.exp2(z - carry - cs), allowed)
    pv = lax.dot_general(w.astype(BF16), vt_blk, _NT, preferred_element_type=F32)
    return carry + jnp.broadcast_to(cs[:, 0:1], carry.shape), pv


def _sb_prompt_kernel(bias_ref, q_ref, k_ref, v_ref, sg_ref, uu_ref, y_ref, qm_ref, carry_ref, acc_ref):
    pair = pl.program_id(1)
    qi = pl.program_id(2)
    nsub = SB_TQ // SB_TK
    q = q_ref[...]
    uu = uu_ref[...]
    lane = lax.broadcasted_iota(jnp.int32, (SB_TQ, LANES), 1)
    for h in range(2):
        qm_ref[h] = q * jnp.where(lane // SB_HEAD_DIM == h, 1.0, 0.0).astype(BF16)
    carry_ref[...] = jnp.zeros_like(carry_ref)
    acc_ref[...] = jnp.zeros_like(acc_ref)
    bias2 = [bias_ref[2 * pair + h] * LOG2E for h in range(2)]

    row = lax.broadcasted_iota(jnp.int32, (SB_TK, SB_TK), 0)
    col = lax.broadcasted_iota(jnp.int32, (SB_TK, SB_TK), 1)
    causal = col < row

    def sweep(tasks):
        masks = [causal if on_diagonal else None for _, _, on_diagonal in tasks]
        scores = [[_sb_scores(qm_ref[h, first:, :], k_ref[kb], bias2[h], uu, allowed) for h in range(2)]
                  for (kb, first, _), allowed in zip(tasks, masks)]
        for (kb, first, _), allowed, per_head in zip(tasks, masks, scores):
            vt = v_ref[kb]
            for h, (z, cs) in enumerate(per_head):
                carry, pv = _sb_apply(z, cs, carry_ref[h, first:, :], vt, allowed)
                carry_ref[h, first:, :] = carry
                acc_ref[h, first:, :] += pv

    for j0 in reversed(range(0, nsub, SB_UNROLL)):
        sweep([(qi * nsub + j, j * SB_TK, True) for j in reversed(range(j0, j0 + SB_UNROLL))])

    def body(i, _):
        kb = qi * nsub - 1 - SB_UNROLL * i
        sweep([(kb - u, 0, False) for u in range(SB_UNROLL)])
        return 0

    lax.fori_loop(0, qi * (nsub // SB_UNROLL), body, 0)
    o = jnp.where(lane < SB_HEAD_DIM, acc_ref[0], acc_ref[1])
    y_ref[...] = (o * _silu(sg_ref[...])).astype(BF16)


def _sb_prompt_call(q_bf, kt_bf, vt_bf, sg, bias, uu):
    b, s, _ = q_bf.shape
    nq = s // SB_TQ
    npair = SB_WIDTH // LANES
    qspec = pl.BlockSpec((None, SB_TQ, LANES), lambda bi, p, i: (bi, i, p))
    kvspec = pl.BlockSpec((None, s // SB_TK, LANES, SB_TK), lambda bi, p, i: (bi, 0, p, 0))
    return pl.pallas_call(
        _sb_prompt_kernel,
        grid=(b, npair, nq),
        in_specs=[pl.BlockSpec(memory_space=pltpu.SMEM), qspec, kvspec, kvspec, qspec,
                  pl.BlockSpec((SB_TK, SB_TK), lambda bi, p, i: (0, 0))],
        out_specs=qspec,
        out_shape=jax.ShapeDtypeStruct((b, s, SB_WIDTH), BF16),
        scratch_shapes=[pltpu.VMEM((2, SB_TQ, LANES), BF16),
                        pltpu.VMEM((2, SB_TQ, LANES), F32),
                        pltpu.VMEM((2, SB_TQ, LANES), F32)],
        compiler_params=_params(("arbitrary", "arbitrary", "arbitrary")),
        name="sb_prompt",
    )(bias, q_bf, kt_bf, vt_bf, sg, uu)


GDN_TS = 512
GDN_KCHUNK = 128
GDN_NCH = GDN_TS // GDN_KCHUNK
GDN_INV_BASE = 8


def _chunk_cumsum_matrix(ts):
    j = lax.broadcasted_iota(jnp.int32, (ts, ts), 0)
    s = lax.broadcasted_iota(jnp.int32, (ts, ts), 1)
    return jnp.where((j <= s) & (j // GDN_KCHUNK == s // GDN_KCHUNK), 1.0, 0.0).astype(BF16)


def _gdn_out(o, gz, norm_w):
    y = o * lax.rsqrt(jnp.mean(o * o, axis=-1, keepdims=True) + RMS_EPS) * norm_w
    return y * _silu(gz)


def _dot(a, b):
    return jnp.dot(a.astype(BF16), b.astype(BF16), preferred_element_type=F32)


def _dot_nt(a, b):
    return lax.dot_general(a.astype(BF16), b.astype(BF16), _NT, preferred_element_type=F32)


def _gdn_prompt_kernel(qkv_ref, gz_ref, gba_ref, gbat_ref, hrow_ref, hcol_ref, nw_ref,
                       ub_ref, lb_ref, y_ref, rec_ref, st_ref):
    si = pl.program_id(1)
    c = GDN_KCHUNK
    dk = GDN_HEAD_DIM

    @pl.when(si == 0)
    def _():
        st_ref[...] = jnp.zeros_like(st_ref)

    qkv = qkv_ref[...]

    gba = gba_ref[...]
    beta_c = jax.nn.sigmoid(gba)
    g_c = -jnp.exp(hrow_ref[0:1, :]) * _softplus(gba + hrow_ref[1:2, :])
    gbat = gbat_ref[...]
    g_r = -jnp.exp(hcol_ref[:, 0:1]) * _softplus(gbat + hcol_ref[:, 1:2])
    ub = ub_ref[...]
    g_hi, g_lo = _split_bf16(g_c)
    lb = lb_ref[...]
    gc_c = (jnp.dot(lb, g_hi, preferred_element_type=F32)
            + jnp.dot(lb, g_lo, preferred_element_type=F32))
    r_hi, r_lo = _split_bf16(g_r)
    gc_r = (jnp.dot(r_hi, ub, preferred_element_type=F32)
            + jnp.dot(r_lo, ub, preferred_element_type=F32))

    ri = lax.broadcasted_iota(jnp.int32, (c, c), 0)
    ci = lax.broadcasted_iota(jnp.int32, (c, c), 1)
    causal = ci <= ri
    strict = ci < ri
    nw = nw_ref[...]
    gz = gz_ref[...]

    items = [(ch, h) for ch in range(GDN_NCH) for h in range(GDN_HEADS)]
    qn, kn, vv = {}, {}, {}
    for h in range(GDN_HEADS):
        lo = h * dk
        qn[h] = qkv[:, lo:lo + dk]
        kn[h] = qkv[:, GDN_WIDTH + lo:GDN_WIDTH + lo + dk]
        vv[h] = qkv[:, 2 * GDN_WIDTH + lo:2 * GDN_WIDTH + lo + dk]
    rows = lambda ch: slice(ch * c, (ch + 1) * c)
    kk = {it: kn[it[1]][rows(it[0])] for it in items}
    qq = {it: qn[it[1]][rows(it[0])] for it in items}
    bcol = {it: beta_c[rows(it[0]), it[1]:it[1] + 1] for it in items}
    gcol = {it: gc_c[rows(it[0]), GDN_HEADS + it[1]:GDN_HEADS + it[1] + 1] for it in items}
    grow = {it: gc_r[GDN_HEADS + it[1]:GDN_HEADS + it[1] + 1, rows(it[0])] for it in items}
    decay = {it: jnp.where(causal, jnp.exp(jnp.where(causal, gcol[it] - grow[it], 0.0)), 0.0)
             for it in items}
    kb = {it: kk[it] * bcol[it] for it in items}
    kk_bf = {it: kk[it].astype(BF16) for it in items}
    x = {it: -jnp.where(strict, _dot_nt(kb[it], kk_bf[it]) * decay[it], 0.0) for it in items}
    blk = lambda w: (ri // w) == (ci // w)
    n = {it: jnp.where(blk(GDN_INV_BASE), x[it], 0.0) for it in items}
    p = dict(n)
    for _ in range(GDN_INV_BASE.bit_length() - 2):
        p = {it: _dot(p[it], p[it]) for it in items}
        n = {it: n[it] + p[it] + _dot(n[it], p[it]) for it in items}
    w = GDN_INV_BASE
    while w < c:
        merge = blk(2 * w) & jnp.logical_not(blk(w))
        off = {it: jnp.where(merge, x[it], 0.0) for it in items}
        yy = {it: off[it] + _dot(n[it], off[it]) for it in items}
        n = {it: n[it] + yy[it] + _dot(yy[it], n[it]) for it in items}
        w *= 2
    egc = {it: jnp.exp(gcol[it]) for it in items}
    rhs = {it: jnp.concatenate([vv[it[1]][rows(it[0])] * bcol[it], kb[it] * egc[it]], axis=1)
           for it in items}
    uw = {it: rhs[it] + _dot(n[it], rhs[it]) for it in items}
    intra = {it: jnp.where(causal, _dot_nt(qq[it], kk_bf[it]) * decay[it], 0.0).astype(BF16)
             for it in items}
    qg = {it: (qq[it] * egc[it]).astype(BF16) for it in items}
    glast = {it: grow[it][:, c - 1:c] for it in items}
    kgl = {it: (kk[it] * jnp.exp(glast[it] - gcol[it])).astype(BF16) for it in items}

    state = {h: st_ref[h] for h in range(GDN_HEADS)}
    for ch in range(GDN_NCH):
        its = [(ch, h) for h in range(GDN_HEADS)]
        sb = {it: state[it[1]].astype(BF16) for it in its}
        v_new = {it: (uw[it][:, :dk] - _dot(uw[it][:, dk:], sb[it])).astype(BF16) for it in its}
        o = {it: jnp.dot(qg[it], sb[it], preferred_element_type=F32)
             + jnp.dot(intra[it], v_new[it], preferred_element_type=F32) for it in its}
        for it in its:
            h = it[1]
            state[h] = (state[h] * jnp.exp(glast[it])
                        + lax.dot_general(kgl[it], v_new[it], _TN, preferred_element_type=F32))
            y_ref[rows(ch), h * dk:(h + 1) * dk] = _gdn_out(
                o[it], gz[rows(ch), h * dk:(h + 1) * dk], nw).astype(BF16)
    for h in range(GDN_HEADS):
        st_ref[h] = state[h]

    @pl.when(si == pl.num_programs(1) - 1)
    def _():
        rec_ref[...] = st_ref[...]


def _gdn_prompt_call(qkv, gz, gba, gbat, hrow, hcol, nw, ub, lb):
    b, s, _ = qkv.shape
    ns = s // GDN_TS
    const2 = lambda shape: pl.BlockSpec(shape, lambda bi, i: (0, 0))
    return pl.pallas_call(
        _gdn_prompt_kernel,
        grid=(b, ns),
        in_specs=[
            pl.BlockSpec((None, GDN_TS, GDN_CONV_DIM), lambda bi, i: (bi, i, 0)),
            pl.BlockSpec((None, GDN_TS, GDN_WIDTH), lambda bi, i: (bi, i, 0)),
            pl.BlockSpec((None, GDN_TS, LANES), lambda bi, i: (bi, i, 0)),
            pl.BlockSpec((None, SUBLANES, GDN_TS), lambda bi, i: (bi, 0, i)),
            const2((2, LANES)), const2((SUBLANES, 2)),
            const2((1, GDN_HEAD_DIM)), const2((GDN_TS, GDN_TS)), const2((GDN_TS, GDN_TS)),
        ],
        out_specs=[
            pl.BlockSpec((None, GDN_TS, GDN_WIDTH), lambda bi, i: (bi, i, 0)),
            pl.BlockSpec((None, GDN_HEADS, GDN_HEAD_DIM, GDN_HEAD_DIM), lambda bi, i: (bi, 0, 0, 0)),
        ],
        out_shape=[
            jax.ShapeDtypeStruct((b, s, GDN_WIDTH), BF16),
            jax.ShapeDtypeStruct((b, GDN_HEADS, GDN_HEAD_DIM, GDN_HEAD_DIM), F32),
        ],
        scratch_shapes=[pltpu.VMEM((GDN_HEADS, GDN_HEAD_DIM, GDN_HEAD_DIM), F32)],
        compiler_params=_params(("arbitrary", "arbitrary")),
        name="gdn_prompt",
    )(qkv, gz, gba, gbat, hrow, hcol, nw, ub, lb)


def _outproj_kernel(ysb_ref, ygdn_ref, x_ref, gate_ref, wt_ref, wb_ref, g_ref, b_ref, o_ref):
    mixed = (jnp.dot(ysb_ref[...], wt_ref[...], preferred_element_type=F32)
             + jnp.dot(ygdn_ref[...], wb_ref[...], preferred_element_type=F32))
    r = DN_ALPHA * x_ref[...] + gate_ref[...] * mixed
    mu = jnp.mean(r, axis=-1, keepdims=True)
    rc = r - mu
    var = jnp.mean(rc * rc, axis=-1, keepdims=True)
    o_ref[...] = rc * lax.rsqrt(var + LN_EPS) * g_ref[...] + b_ref[...]


def _outproj_call(y_sb, y_gdn, x, gate, w_top, w_bot, ln_g, ln_b, tm, per_row_mod):
    g, t, _ = x.shape
    nt = t // tm
    if per_row_mod:
        gate_spec = pl.BlockSpec((None, tm, D_MODEL), lambda b, i: (b, i, 0))
    else:
        gate_spec = pl.BlockSpec((None, 1, D_MODEL), lambda b, i: (b, 0, 0))
    half = pl.BlockSpec((None, tm, SB_WIDTH), lambda b, i: (b, i, 0))
    full = pl.BlockSpec((None, tm, D_MODEL), lambda b, i: (b, i, 0))
    const = lambda shape: pl.BlockSpec(shape, lambda b, i: (0, 0))
    return pl.pallas_call(
        _outproj_kernel,
        grid=(g, nt),
        in_specs=[half, half, full, gate_spec, const((SB_WIDTH, D_MODEL)), const((GDN_WIDTH, D_MODEL)),
                  const((1, D_MODEL)), const((1, D_MODEL))],
        out_specs=full,
        out_shape=jax.ShapeDtypeStruct((g, t, D_MODEL), F32),
        compiler_params=_params(("arbitrary", "arbitrary")),
        name="out_proj_ln",
    )(y_sb, y_gdn, x, gate, w_top, w_bot, ln_g, ln_b)


def _gdn_step_kernel(qkv_ref, cs_ref, gz_ref, gba_ref, rec_ref, cw_ref, hrow_ref, nw_ref,
                     y_ref, cs_out_ref, rec_out_ref):
    cdim = GDN_CONV_DIM
    x = qkv_ref[...]
    y = x * cw_ref[CONV_W - 1:CONV_W, :]
    for i in range(CONV_W - 1):
        y = y + cs_ref[:, i * cdim:(i + 1) * cdim] * cw_ref[i:i + 1, :]
    qkv = _silu(y)
    for i in range(CONV_W - 2):
        cs_out_ref[:, i * cdim:(i + 1) * cdim] = cs_ref[:, (i + 1) * cdim:(i + 2) * cdim]
    cs_out_ref[:, (CONV_W - 2) * cdim:(CONV_W - 1) * cdim] = x

    gba = gba_ref[...]
    beta = jax.nn.sigmoid(gba)
    g = -jnp.exp(hrow_ref[0:1, :]) * _softplus(gba + hrow_ref[1:2, :])
    eye = (lax.broadcasted_iota(jnp.int32, (GDN_HEAD_DIM, GDN_HEAD_DIM), 0)
           == lax.broadcasted_iota(jnp.int32, (GDN_HEAD_DIM, GDN_HEAD_DIM), 1))

    def column(row):
        return jnp.sum(jnp.where(eye, row, 0.0), axis=1, keepdims=True)

    nw = nw_ref[...]
    gz = gz_ref[...]
    for h in range(GDN_HEADS):
        lo = h * GDN_HEAD_DIM
        hi = lo + GDN_HEAD_DIM
        q_h = _l2norm(qkv[:, lo:hi]) * GDN_HEAD_DIM ** -0.5
        k_h = _l2norm(qkv[:, GDN_WIDTH + lo:GDN_WIDTH + hi])
        v_h = qkv[:, 2 * GDN_WIDTH + lo:2 * GDN_WIDTH + hi]
        s = rec_ref[h] * jnp.exp(g[:, GDN_HEADS + h:GDN_HEADS + h + 1])
        kcol = column(k_h)
        delta = (v_h - jnp.sum(s * kcol, axis=0, keepdims=True)) * beta[:, h:h + 1]
        s = s + kcol * delta
        rec_out_ref[h] = s
        o = jnp.sum(s * column(q_h), axis=0, keepdims=True)
        y_ref[:, lo:hi] = _gdn_out(o, gz[:, lo:hi], nw).astype(BF16)


def _gdn_step_call(layer, qkv, conv_state, gz, gba, state_rec, cw, hrow, nw):
    db = qkv.shape[0]
    row = lambda width: pl.BlockSpec((None, 1, width), lambda b: (b, 0, 0))
    const = lambda shape: pl.BlockSpec(shape, lambda b: (0, 0))
    rec_shape = (GDN_HEADS, GDN_HEAD_DIM, GDN_HEAD_DIM)
    cs_width = (CONV_W - 1) * GDN_CONV_DIM
    return pl.pallas_call(
        _gdn_step_kernel,
        grid=(db,),
        in_specs=[row(GDN_CONV_DIM), row(cs_width), row(GDN_WIDTH), row(LANES),
                  pl.BlockSpec((None, None) + rec_shape, lambda b: (layer, b, 0, 0, 0)),
                  const((CONV_W, GDN_CONV_DIM)), const((2, LANES)), const((1, GDN_HEAD_DIM))],
        out_specs=[row(GDN_WIDTH), row(cs_width),
                   pl.BlockSpec((None,) + rec_shape, lambda b: (b, 0, 0, 0))],
        out_shape=[jax.ShapeDtypeStruct((db, 1, GDN_WIDTH), BF16),
                   jax.ShapeDtypeStruct((db, 1, cs_width), F32),
                   jax.ShapeDtypeStruct((db,) + rec_shape, F32)],
        compiler_params=_params(("arbitrary",)),
        name="gdn_step",
    )(qkv, conv_state, gz, gba, state_rec, cw, hrow, nw)


PAGES_PER_STEP = 32


def _sb_paged_kernel(pt_ref, q_ref, kn_ref, vn_ref, sg_ref, bias_ref, uu_ref, *rest):
    del pt_ref
    np_ = PAGES_PER_STEP
    k_refs = rest[:np_]
    v_refs = rest[np_:2 * np_]
    y_ref, qm_ref, carry_ref, acc_ref = rest[2 * np_:]
    gi = pl.program_id(1)
    head = lax.broadcasted_iota(jnp.int32, (SB_HEADS, SB_WIDTH), 0)
    chan = lax.broadcasted_iota(jnp.int32, (SB_HEADS, SB_WIDTH), 1)
    own = chan // SB_HEAD_DIM == head

    @pl.when(gi == 0)
    def _():
        qm_ref[...] = jnp.where(own, q_ref[...].astype(F32), 0.0)
        carry_ref[...] = jnp.zeros_like(carry_ref)
        acc_ref[...] = jnp.zeros_like(acc_ref)

    qm32 = qm_ref[...]
    qm = qm32.astype(BF16)
    bias = bias_ref[...] * LOG2E
    tk = SB_TK
    kt_all = jnp.concatenate([r[...].astype(BF16) for r in k_refs], axis=1)
    z = jnp.dot(qm, kt_all, preferred_element_type=F32) + bias
    sp = _softplus2(z)
    hi, lo = _split_bf16(jnp.concatenate([sp[:, i * tk:(i + 1) * tk] for i in range(np_)], axis=0))
    cs = jnp.dot(jnp.concatenate([hi, lo], axis=1), uu_ref[...], preferred_element_type=F32)
    carry = carry_ref[...]
    ws = [None] * np_
    for i in reversed(range(np_)):
        cs_i = cs[i * SB_HEADS:(i + 1) * SB_HEADS]
        ws[i] = jnp.exp2(z[:, i * tk:(i + 1) * tk] - carry - cs_i[:, :tk])
        carry = carry + cs_i[:, tk:]
    carry_ref[...] = carry
    vt_all = jnp.concatenate([r[...].astype(BF16) for r in v_refs], axis=1)
    acc = acc_ref[...] + lax.dot_general(jnp.concatenate(ws, axis=1).astype(BF16), vt_all, _NT,
                                         preferred_element_type=F32)
    acc_ref[...] = acc

    @pl.when(gi == pl.num_programs(1) - 1)
    def _():
        t_new = kn_ref.shape[0]
        z_new = jnp.sum(qm32 * kn_ref[...], axis=-1, keepdims=True) + bias
        q_idx = lax.broadcasted_iota(jnp.int32, (SB_HEADS, t_new), 1) + (t_new - 1)
        k_idx = lax.broadcasted_iota(jnp.int32, (SB_HEADS, t_new), 1)
        w_new = jnp.where(k_idx < q_idx, jnp.exp2(z_new - _softplus(z_new / LOG2E) * LOG2E), 0.0)
        total = acc + w_new * vn_ref[...]
        o = jnp.sum(jnp.where(own, total, 0.0), axis=0, keepdims=True)
        y_ref[...] = (o * _silu(sg_ref[...])).astype(BF16)


def _sb_paged_call(layer, page_table, q_bf, k_new, v_new, sg, bias_col, uu, cache_k, cache_v):
    db, n_pages = page_table.shape
    np_ = PAGES_PER_STEP
    ng = n_pages // np_
    page = cache_k.shape[2]
    ck = jnp.transpose(cache_k, (0, 1, 3, 4, 2)).reshape(cache_k.shape[0], cache_k.shape[1], SB_WIDTH, page)
    cv = jnp.transpose(cache_v, (0, 1, 3, 4, 2)).reshape(cache_v.shape[0], cache_v.shape[1], SB_WIDTH, page)
    row = pl.BlockSpec((None, 1, SB_WIDTH), lambda b, g, pt: (b, 0, 0))

    def page_spec(i):
        return pl.BlockSpec(
            (None, None, SB_WIDTH, page),
            lambda b, g, pt: (layer, pt[b, (ng - 1 - g) * np_ + i], 0, 0))

    grid_spec = pltpu.PrefetchScalarGridSpec(
        num_scalar_prefetch=1,
        grid=(db, ng),
        in_specs=[row, row, row, row,
                  pl.BlockSpec((SB_HEADS, 1), lambda b, g, pt: (0, 0)),
                  pl.BlockSpec((2 * SB_TK, 2 * SB_TK), lambda b, g, pt: (0, 0))]
                 + [page_spec(i) for i in range(np_)] * 2,
        out_specs=row,
        scratch_shapes=[pltpu.VMEM((SB_HEADS, SB_WIDTH), F32),
                        pltpu.VMEM((SB_HEADS, SB_TK), F32),
                        pltpu.VMEM((SB_HEADS, SB_WIDTH), F32)],
    )
    return pl.pallas_call(
        _sb_paged_kernel,
        grid_spec=grid_spec,
        out_shape=jax.ShapeDtypeStruct((db, 1, SB_WIDTH), BF16),
        compiler_params=_params(("arbitrary", "arbitrary")),
        name="sb_paged",
    )(page_table, q_bf, k_new, v_new, sg, bias_col, uu, *([ck] * np_), *([cv] * np_))


def _pick_tile(t, cap):
    tm = min(t, cap)
    assert t % tm == 0
    return tm


def kernel(x_prompt, x_sample, cache_k, cache_v, page_table, state_conv, state_rec, c_prompt, c_sample,
           w_mod, b_mod, w_in, sb_bias, conv_w, a_log, dt_bias, gdn_norm_w, w_out, ln_g, ln_b):
    b, s, _ = x_prompt.shape
    db, t_new, _ = x_sample.shape
    assert t_new == 1 and s % GDN_TS == 0 and s % SB_TQ == 0
    assert page_table.shape[1] % PAGES_PER_STEP == 0 and cache_k.shape[2] == SB_TK

    mod = _mod_call(jnp.concatenate([c_sample, c_prompt], axis=0), w_mod, b_mod)
    mod_rows = mod.reshape(DEPTH, 3, db + b, 1, D_MODEL)

    uu = _cumsum_matrix()
    ub = _chunk_cumsum_matrix(GDN_TS)
    lb = ub.T
    gate_pad = ((0, 0), (GDN_HEADS, LANES - 2 * GDN_HEADS))

    hp, hs = x_prompt, x_sample.reshape(1, db, D_MODEL)
    cp_l, rp_l, ks_l, vs_l, cs_l, rs_l = [], [], [], [], [], []
    kv_prompt = ()
    for l in range(DEPTH):
        w_bf = jnp.pad(w_in[l], ((0, 0), (0, W_IN_PAD - w_in.shape[2]))).astype(BF16)
        wt_bf = jnp.concatenate(
            [w_in[l][:, COL_K:COL_G], w_in[l][:, COL_BA:COL_BA + SUBLANES]], axis=1).T.astype(BF16)
        w_top = w_out[l][:SB_WIDTH].astype(BF16)
        w_bot = w_out[l][SB_WIDTH:].astype(BF16)
        hrow = jnp.pad(jnp.stack([a_log[l], dt_bias[l]]), gate_pad)
        hcol = hrow[:, :SUBLANES].T
        nw = gdn_norm_w[l].reshape(1, GDN_HEAD_DIM)
        lng = ln_g[l].reshape(1, D_MODEL)
        lnb = ln_b[l].reshape(1, D_MODEL)

        shift, scale, gate = (mod_rows[l, i, db:] for i in range(3))
        tm = _pick_tile(s, 256)
        q_bf, kt, vt, kt_bf, vt_bf, sg, qkv, gz, gba, gbat, conv_p = _inproj_call(
            hp, shift, scale, w_bf, wt_bf, tm, per_row_mod=False, transposed_kv=True,
            layer=l, kv_buffers=kv_prompt, conv_w=conv_w[l])
        kv_prompt = (kt, vt)
        y_sb = _sb_prompt_call(q_bf, kt_bf, vt_bf, sg, sb_bias[l], uu[:SB_TK, :SB_TK])
        y_gdn, rec_p = _gdn_prompt_call(qkv, gz, gba, gbat, hrow, hcol, nw, ub, lb)
        hp = _outproj_call(y_sb, y_gdn, hp, gate, w_top, w_bot, lng, lnb, _pick_tile(s, 512), False)
        cp_l.append(conv_p)
        rp_l.append(rec_p)

        shift, scale, gate = (mod[l, i, :db].reshape(1, db, D_MODEL) for i in range(3))
        q_bf, k, v, sg, qkv, gz, gba = _inproj_call(
            hs, shift, scale, w_bf, wt_bf, db, per_row_mod=True, transposed_kv=False)
        as_rows = lambda a: a.reshape(db, 1, a.shape[-1])
        y_sb = _sb_paged_call(l, page_table, as_rows(q_bf), as_rows(k), as_rows(v), as_rows(sg),
                              sb_bias[l].reshape(SB_HEADS, 1), uu, cache_k, cache_v)
        y_gdn, conv_s, rec_s = _gdn_step_call(
            l, as_rows(qkv), state_conv[l].reshape(db, 1, (CONV_W - 1) * GDN_CONV_DIM),
            as_rows(gz), as_rows(gba), state_rec, conv_w[l], hrow, nw)
        hs = _outproj_call(y_sb.reshape(1, db, SB_WIDTH), y_gdn.reshape(1, db, GDN_WIDTH), hs, gate,
                           w_top, w_bot, lng, lnb, db, True)
        ks_l.append(k.reshape(db, 1, SB_HEADS, SB_HEAD_DIM))
        vs_l.append(v.reshape(db, 1, SB_HEADS, SB_HEAD_DIM))
        cs_l.append(conv_s.reshape(db, CONV_W - 1, GDN_CONV_DIM))
        rs_l.append(rec_s)

    heads_last = lambda a: jnp.transpose(a.reshape(DEPTH, b, SB_HEADS, SB_HEAD_DIM, s), (0, 1, 4, 2, 3))
    k_prompt, v_prompt = (heads_last(a) for a in kv_prompt)
    return (hp, hs.reshape(db, 1, D_MODEL), k_prompt, v_prompt, jnp.stack(cp_l),
            jnp.stack(rp_l), jnp.stack(ks_l), jnp.stack(vs_l), jnp.stack(cs_l), jnp.stack(rs_l))
```

```python
import functools

import jax
import jax.numpy as jnp
from jax import lax
from jax.experimental import pallas as pl
from jax.experimental.pallas import tpu as pltpu

F32 = jnp.float32
BF16 = jnp.bfloat16

D_MODEL = 1024
DEPTH = 2
SB_HEADS = 8
SB_HEAD_DIM = 64
SB_WIDTH = SB_HEADS * SB_HEAD_DIM
SB_SCALE = SB_HEAD_DIM ** -0.5
GDN_HEADS = 4
GDN_HEAD_DIM = 128
GDN_WIDTH = GDN_HEADS * GDN_HEAD_DIM
GDN_CONV_DIM = 3 * GDN_WIDTH
CONV_W = 4
DN_ALPHA = (2 * DEPTH) ** 0.25
LN_EPS = 1e-5
RMS_EPS = 1e-6
L2_EPS = 1e-6

LANES = 128
SUBLANES = 8
COL_Q, COL_K, COL_V, COL_G = 0, SB_WIDTH, 2 * SB_WIDTH, 3 * SB_WIDTH
COL_QKV = 4 * SB_WIDTH
COL_Z = COL_QKV + GDN_CONV_DIM
COL_BA = COL_Z + GDN_WIDTH
W_IN_PAD = COL_BA + LANES
VMEM_LIMIT = 56 * 1024 * 1024

_NT = (((1,), (1,)), ((), ()))
_TN = (((0,), (0,)), ((), ()))


def _silu(x):
    return x * jax.nn.sigmoid(x)


def _softplus(x):
    return jnp.maximum(x, 0.0) + jnp.log1p(jnp.exp(-jnp.abs(x)))


def _split_bf16(x):
    hi = x.astype(BF16)
    lo = (x - hi.astype(F32)).astype(BF16)
    return hi, lo


def _params(sem):
    return pltpu.CompilerParams(dimension_semantics=sem, vmem_limit_bytes=VMEM_LIMIT)


def _mod_kernel(c_ref, w_ref, b_ref, o_ref):
    a = _silu(c_ref[...]).astype(BF16)
    o_ref[...] = jnp.dot(a, w_ref[...].astype(BF16), preferred_element_type=F32) + b_ref[...]


def _mod_call(c_all, w_mod, b_mod):
    n = c_all.shape[0]
    b4 = b_mod.reshape(DEPTH, 3, 1, D_MODEL)
    return pl.pallas_call(
        _mod_kernel,
        grid=(DEPTH, 3),
        in_specs=[
            pl.BlockSpec((n, D_MODEL), lambda l, p: (0, 0)),
            pl.BlockSpec((None, D_MODEL, D_MODEL), lambda l, p: (l, 0, p)),
            pl.BlockSpec((None, None, 1, D_MODEL), lambda l, p: (l, p, 0, 0)),
        ],
        out_specs=pl.BlockSpec((None, None, n, D_MODEL), lambda l, p: (l, p, 0, 0)),
        out_shape=jax.ShapeDtypeStruct((DEPTH, 3, n, D_MODEL), F32),
        compiler_params=_params(("arbitrary", "arbitrary")),
        name="adaln_mod",
    )(c_all, w_mod, b4)


SB_TQ = 1024
SB_UNROLL = 8
SB_TK = 128
LOG2E = 1.4426950408889634


def _l2norm(x):
    return x * lax.rsqrt(jnp.sum(x * x, axis=-1, keepdims=True) + L2_EPS)


def _inproj_kernel(transposed_kv, n_aliased, x_ref, shift_ref, scale_ref, w_ref, wt_ref, *refs):
    refs = refs[n_aliased:]
    u = (x_ref[...] * (1.0 + scale_ref[...]) + shift_ref[...]).astype(BF16)

    def seg(lo, width):
        return jnp.dot(u, w_ref[:, lo:lo + width], preferred_element_type=F32)

    if transposed_kv:
        (cw_ref, q_ref, kt_ref, vt_ref, ktb_ref, vtb_ref, sg_ref, qkv_ref, gz_ref, gba_ref, gbat_ref,
         conv_ref, xp_ref) = refs
        t = lax.dot_general(wt_ref[...], u, _NT, preferred_element_type=F32)
        kt = t[:SB_WIDTH]
        vt = t[SB_WIDTH:2 * SB_WIDTH]
        kt_ref[...] = kt
        vt_ref[...] = vt
        for j in range(u.shape[0] // SB_TK):
            ktb_ref[j] = kt[:, j * SB_TK:(j + 1) * SB_TK].astype(BF16)
            vtb_ref[j] = vt[:, j * SB_TK:(j + 1) * SB_TK].astype(BF16)
        gbat_ref[...] = t[2 * SB_WIDTH:]
    else:
        q_ref, k_ref, v_ref, sg_ref, qkv_ref, gz_ref, gba_ref = refs
        k_ref[...] = seg(COL_K, SB_WIDTH)
        v_ref[...] = seg(COL_V, SB_WIDTH)
    q_ref[...] = (seg(COL_Q, SB_WIDTH) * (SB_SCALE * LOG2E)).astype(BF16)
    sg_ref[...] = seg(COL_G, SB_WIDTH)
    gz_ref[...] = seg(COL_Z, GDN_WIDTH)
    gba_ref[...] = seg(COL_BA, LANES)
    if not transposed_kv:
        for j in range(3):
            qkv_ref[:, j * GDN_WIDTH:(j + 1) * GDN_WIDTH] = seg(COL_QKV + j * GDN_WIDTH, GDN_WIDTH)
        return

    ti = pl.program_id(1)
    tm = u.shape[0]
    pad = SUBLANES

    @pl.when(ti == 0)
    def _():
        xp_ref[0:pad, :] = jnp.zeros((pad, GDN_CONV_DIM), F32)

    for j in range(3):
        xp_ref[pad:pad + tm, j * GDN_WIDTH:(j + 1) * GDN_WIDTH] = seg(COL_QKV + j * GDN_WIDTH, GDN_WIDTH)
    y = xp_ref[pad:pad + tm, :] * cw_ref[CONV_W - 1:CONV_W, :]
    for i in range(CONV_W - 1):
        sh = CONV_W - 1 - i
        y = y + xp_ref[pad - sh:pad - sh + tm, :] * cw_ref[i:i + 1, :]
    qkv = _silu(y)
    tail = xp_ref[tm:tm + pad, :]
    xp_ref[0:pad, :] = tail
    for h in range(GDN_HEADS):
        lo = h * GDN_HEAD_DIM
        hi = lo + GDN_HEAD_DIM
        qkv_ref[:, lo:hi] = _l2norm(qkv[:, lo:hi]) * GDN_HEAD_DIM ** -0.5
        qkv_ref[:, GDN_WIDTH + lo:GDN_WIDTH + hi] = _l2norm(qkv[:, GDN_WIDTH + lo:GDN_WIDTH + hi])
    qkv_ref[:, 2 * GDN_WIDTH:] = qkv[:, 2 * GDN_WIDTH:]

    @pl.when(ti == pl.num_programs(1) - 1)
    def _():
        conv_ref[...] = tail[pad - (CONV_W - 1):pad, :]


def _inproj_call(x, shift, scale, w_bf, wt_bf, tm, per_row_mod, transposed_kv, layer=0, kv_buffers=(),
                 conv_w=None):
    g, t, _ = x.shape
    nt = t // tm
    if per_row_mod:
        mod_spec = pl.BlockSpec((None, tm, D_MODEL), lambda b, i: (b, i, 0))
    else:
        mod_spec = pl.BlockSpec((None, 1, D_MODEL), lambda b, i: (b, 0, 0))

    def tok(width, dtype):
        return (pl.BlockSpec((None, tm, width), lambda b, i: (b, i, 0)),
                jax.ShapeDtypeStruct((g, t, width), dtype))

    def chan(rows):
        return (pl.BlockSpec((None, rows, tm), lambda b, i: (b, 0, i)),
                jax.ShapeDtypeStruct((g, rows, t), F32))

    def blocks():
        return (pl.BlockSpec((None, tm // SB_TK, SB_WIDTH, SB_TK), lambda b, i: (b, i, 0, 0)),
                jax.ShapeDtypeStruct((g, t // SB_TK, SB_WIDTH, SB_TK), BF16))

    def layered():
        return (pl.BlockSpec((None, None, SB_WIDTH, tm), lambda b, i: (layer, b, 0, i)),
                jax.ShapeDtypeStruct((DEPTH, g, SB_WIDTH, t), F32))

    tail = [tok(SB_WIDTH, F32), tok(GDN_CONV_DIM, F32), tok(GDN_WIDTH, F32), tok(LANES, F32)]
    if transposed_kv:
        outs = [tok(SB_WIDTH, BF16), layered(), layered(), blocks(), blocks()] + tail
        outs.append(chan(SUBLANES))
        outs.append((pl.BlockSpec((None, CONV_W - 1, GDN_CONV_DIM), lambda b, i: (b, 0, 0)),
                     jax.ShapeDtypeStruct((g, CONV_W - 1, GDN_CONV_DIM), F32)))
        extra_in = [conv_w]
        extra_specs = [pl.BlockSpec((CONV_W, GDN_CONV_DIM), lambda b, i: (0, 0))]
        scratch = [pltpu.VMEM((tm + SUBLANES, GDN_CONV_DIM), F32)]
    else:
        outs = [tok(SB_WIDTH, BF16), tok(SB_WIDTH, F32), tok(SB_WIDTH, F32)] + tail
        extra_in, extra_specs, scratch = [], [], []
    n_in = 5
    return pl.pallas_call(
        functools.partial(_inproj_kernel, transposed_kv, len(kv_buffers)),
        grid=(g, nt),
        in_specs=[
            pl.BlockSpec((None, tm, D_MODEL), lambda b, i: (b, i, 0)),
            mod_spec, mod_spec,
            pl.BlockSpec((D_MODEL, W_IN_PAD), lambda b, i: (0, 0)),
            pl.BlockSpec(wt_bf.shape, lambda b, i: (0, 0)),
        ] + [pl.BlockSpec(memory_space=pl.ANY)] * len(kv_buffers) + extra_specs,
        out_specs=[o[0] for o in outs],
        out_shape=[o[1] for o in outs],
        scratch_shapes=scratch,
        input_output_aliases={n_in + j: 1 + j for j in range(len(kv_buffers))},
        compiler_params=_params(("arbitrary", "arbitrary")),
        name="in_proj",
    )(x, shift, scale, w_bf, wt_bf, *kv_buffers, *extra_in)


def _cumsum_matrix():
    j = lax.broadcasted_iota(jnp.int32, (2 * SB_TK, 2 * SB_TK), 0) % SB_TK
    s = lax.broadcasted_iota(jnp.int32, (2 * SB_TK, 2 * SB_TK), 1)
    return jnp.where((s >= SB_TK) | (j >= s), 1.0, 0.0).astype(BF16)


def _neg_abs(x):
    bits = lax.bitcast_convert_type(x, jnp.uint32) | jnp.uint32(0x80000000)
    return lax.bitcast_convert_type(bits, F32)


def _softplus2(z):
    return jnp.maximum(z, 0.0) + jnp.log(1.0 + jnp.exp2(_neg_abs(z))) * LOG2E


def _mask_leading(x, allowed):
    if allowed is None:
        return x
    n = allowed.shape[0]
    head = jnp.where(allowed, x[:n], 0.0)
    return head if x.shape[0] == n else jnp.concatenate([head, x[n:]], axis=0)


def _sb_scores(qh, kt_blk, bias2, u_incl, allowed):
    z = jnp.dot(qh, kt_blk, preferred_element_type=F32) + bias2
    rest = _mask_leading(_softplus2(z), allowed).astype(BF16)
    return z, jnp.dot(rest, u_incl, preferred_element_type=F32)


def _sb_apply(z, cs, carry, vt_blk, allowed):
    w = _mask_leading(jnp.exp2(z - carry - cs), allowed)
    pv = lax.dot_general(w.astype(BF16), vt_blk, _NT, preferred_element_type=F32)
    return carry + jnp.broadcast_to(cs[:, 0:1], carry.shape), pv


def _sb_prompt_kernel(bias_ref, q_ref, k_ref, v_ref, sg_ref, uu_ref, y_ref, qm_ref, carry_ref, acc_ref):
    pair = pl.program_id(1)
    qi = pl.program_id(2)
    nsub = SB_TQ // SB_TK
    q = q_ref[...]
    uu = uu_ref[...]
    lane = lax.broadcasted_iota(jnp.int32, (SB_TQ, LANES), 1)
    for h in range(2):
        qm_ref[h] = q * jnp.where(lane // SB_HEAD_DIM == h, 1.0, 0.0).astype(BF16)
    carry_ref[...] = jnp.zeros_like(carry_ref)
    acc_ref[...] = jnp.zeros_like(acc_ref)
    bias2 = [bias_ref[2 * pair + h] * LOG2E for h in range(2)]

    row = lax.broadcasted_iota(jnp.int32, (SB_TK, SB_TK), 0)
    col = lax.broadcasted_iota(jnp.int32, (SB_TK, SB_TK), 1)
    causal = col < row

    def sweep(tasks):
        masks = [causal if on_diagonal else None for _, _, on_diagonal in tasks]
        scores = [[_sb_scores(qm_ref[h, first:, :], k_ref[kb], bias2[h], uu, allowed) for h in range(2)]
                  for (kb, first, _), allowed in zip(tasks, masks)]
        for (kb, first, _), allowed, per_head in zip(tasks, masks, scores):
            vt = v_ref[kb]
            for h, (z, cs) in enumerate(per_head):
                carry, pv = _sb_apply(z, cs, carry_ref[h, first:, :], vt, allowed)
                carry_ref[h, first:, :] = carry
                acc_ref[h, first:, :] += pv

    for j0 in reversed(range(0, nsub, SB_UNROLL)):
        sweep([(qi * nsub + j, j * SB_TK, True) for j in reversed(range(j0, j0 + SB_UNROLL))])

    def body(i, _):
        kb = qi * nsub - 1 - SB_UNROLL * i
        sweep([(kb - u, 0, False) for u in range(SB_UNROLL)])
        return 0

    lax.fori_loop(0, qi * (nsub // SB_UNROLL), body, 0)
    o = jnp.where(lane < SB_HEAD_DIM, acc_ref[0], acc_ref[1])
    y_ref[...] = (o * _silu(sg_ref[...])).astype(BF16)


def _sb_prompt_call(q_bf, kt_bf, vt_bf, sg, bias, uu):
    b, s, _ = q_bf.shape
    nq = s // SB_TQ
    npair = SB_WIDTH // LANES
    qspec = pl.BlockSpec((None, SB_TQ, LANES), lambda bi, p, i: (bi, i, p))
    kvspec = pl.BlockSpec((None, s // SB_TK, LANES, SB_TK), lambda bi, p, i: (bi, 0, p, 0))
    return pl.pallas_call(
        _sb_prompt_kernel,
        grid=(b, npair, nq),
        in_specs=[pl.BlockSpec(memory_space=pltpu.SMEM), qspec, kvspec, kvspec, qspec,
                  pl.BlockSpec((SB_TK, SB_TK), lambda bi, p, i: (0, 0))],
        out_specs=qspec,
        out_shape=jax.ShapeDtypeStruct((b, s, SB_WIDTH), BF16),
        scratch_shapes=[pltpu.VMEM((2, SB_TQ, LANES), BF16),
                        pltpu.VMEM((2, SB_TQ, LANES), F32),
                        pltpu.VMEM((2, SB_TQ, LANES), F32)],
        compiler_params=_params(("arbitrary", "arbitrary", "arbitrary")),
        name="sb_prompt",
    )(bias, q_bf, kt_bf, vt_bf, sg, uu)


GDN_TS = 512
GDN_KCHUNK = 128
GDN_NCH = GDN_TS // GDN_KCHUNK
GDN_INV_BASE = 8


def _chunk_cumsum_matrix(ts):
    j = lax.broadcasted_iota(jnp.int32, (ts, ts), 0)
    s = lax.broadcasted_iota(jnp.int32, (ts, ts), 1)
    return jnp.where((j <= s) & (j // GDN_KCHUNK == s // GDN_KCHUNK), 1.0, 0.0).astype(BF16)


def _gdn_out(o, gz, norm_w):
    y = o * lax.rsqrt(jnp.mean(o * o, axis=-1, keepdims=True) + RMS_EPS) * norm_w
    return y * _silu(gz)


def _dot(a, b):
    return jnp.dot(a.astype(BF16), b.astype(BF16), preferred_element_type=F32)


def _dot_nt(a, b):
    return lax.dot_general(a.astype(BF16), b.astype(BF16), _NT, preferred_element_type=F32)


def _gdn_prompt_kernel(qkv_ref, gz_ref, gba_ref, gbat_ref, hrow_ref, hcol_ref, nw_ref,
                       ub_ref, lb_ref, y_ref, rec_ref, st_ref):
    si = pl.program_id(1)
    c = GDN_KCHUNK
    dk = GDN_HEAD_DIM

    @pl.when(si == 0)
    def _():
        st_ref[...] = jnp.zeros_like(st_ref)

    qkv = qkv_ref[...]

    gba = gba_ref[...]
    beta_c = jax.nn.sigmoid(gba)
    g_c = -jnp.exp(hrow_ref[0:1, :]) * _softplus(gba + hrow_ref[1:2, :])
    gbat = gbat_ref[...]
    g_r = -jnp.exp(hcol_ref[:, 0:1]) * _softplus(gbat + hcol_ref[:, 1:2])
    ub = ub_ref[...]
    g_hi, g_lo = _split_bf16(g_c)
    lb = lb_ref[...]
    gc_c = (jnp.dot(lb, g_hi, preferred_element_type=F32)
            + jnp.dot(lb, g_lo, preferred_element_type=F32))
    r_hi, r_lo = _split_bf16(g_r)
    gc_r = (jnp.dot(r_hi, ub, preferred_element_type=F32)
            + jnp.dot(r_lo, ub, preferred_element_type=F32))

    ri = lax.broadcasted_iota(jnp.int32, (c, c), 0)
    ci = lax.broadcasted_iota(jnp.int32, (c, c), 1)
    causal = ci <= ri
    strict = ci < ri
    nw = nw_ref[...]
    gz = gz_ref[...]

    items = [(ch, h) for ch in range(GDN_NCH) for h in range(GDN_HEADS)]
    qn, kn, vv = {}, {}, {}
    for h in range(GDN_HEADS):
        lo = h * dk
        qn[h] = qkv[:, lo:lo + dk]
        kn[h] = qkv[:, GDN_WIDTH + lo:GDN_WIDTH + lo + dk]
        vv[h] = qkv[:, 2 * GDN_WIDTH + lo:2 * GDN_WIDTH + lo + dk]
    rows = lambda ch: slice(ch * c, (ch + 1) * c)
    kk = {it: kn[it[1]][rows(it[0])] for it in items}
    qq = {it: qn[it[1]][rows(it[0])] for it in items}
    bcol = {it: beta_c[rows(it[0]), it[1]:it[1] + 1] for it in items}
    gcol = {it: gc_c[rows(it[0]), GDN_HEADS + it[1]:GDN_HEADS + it[1] + 1] for it in items}
    grow = {it: gc_r[GDN_HEADS + it[1]:GDN_HEADS + it[1] + 1, rows(it[0])] for it in items}
    decay = {it: jnp.where(causal, jnp.exp(jnp.where(causal, gcol[it] - grow[it], 0.0)), 0.0)
             for it in items}
    kb = {it: kk[it] * bcol[it] for it in items}
    kk_bf = {it: kk[it].astype(BF16) for it in items}
    x = {it: -jnp.where(strict, _dot_nt(kb[it], kk_bf[it]) * decay[it], 0.0) for it in items}
    blk = lambda w: (ri // w) == (ci // w)
    n = {it: jnp.where(blk(GDN_INV_BASE), x[it], 0.0) for it in items}
    p = dict(n)
    for _ in range(GDN_INV_BASE.bit_length() - 2):
        p = {it: _dot(p[it], p[it]) for it in items}
        n = {it: n[it] + p[it] + _dot(n[it], p[it]) for it in items}
    w = GDN_INV_BASE
    while w < c:
        merge = blk(2 * w) & jnp.logical_not(blk(w))
        off = {it: jnp.where(merge, x[it], 0.0) for it in items}
        yy = {it: off[it] + _dot(n[it], off[it]) for it in items}
        n = {it: n[it] + yy[it] + _dot(yy[it], n[it]) for it in items}
        w *= 2
    egc = {it: jnp.exp(gcol[it]) for it in items}
    rhs = {it: jnp.concatenate([vv[it[1]][rows(it[0])] * bcol[it], kb[it] * egc[it]], axis=1)
           for it in items}
    uw = {it: rhs[it] + _dot(n[it], rhs[it]) for it in items}
    intra = {it: jnp.where(causal, _dot_nt(qq[it], kk_bf[it]) * decay[it], 0.0).astype(BF16)
             for it in items}
    qg = {it: (qq[it] * egc[it]).astype(BF16) for it in items}
    glast = {it: grow[it][:, c - 1:c] for it in items}
    kgl = {it: (kk[it] * jnp.exp(glast[it] - gcol[it])).astype(BF16) for it in items}

    state = {h: st_ref[h] for h in range(GDN_HEADS)}
    for ch in range(GDN_NCH):
        its = [(ch, h) for h in range(GDN_HEADS)]
        sb = {it: state[it[1]].astype(BF16) for it in its}
        v_new = {it: (uw[it][:, :dk] - _dot(uw[it][:, dk:], sb[it])).astype(BF16) for it in its}
        o = {it: jnp.dot(qg[it], sb[it], preferred_element_type=F32)
             + jnp.dot(intra[it], v_new[it], preferred_element_type=F32) for it in its}
        for it in its:
            h = it[1]
            state[h] = (state[h] * jnp.exp(glast[it])
                        + lax.dot_general(kgl[it], v_new[it], _TN, preferred_element_type=F32))
            y_ref[rows(ch), h * dk:(h + 1) * dk] = _gdn_out(
                o[it], gz[rows(ch), h * dk:(h + 1) * dk], nw).astype(BF16)
    for h in range(GDN_HEADS):
        st_ref[h] = state[h]

    @pl.when(si == pl.num_programs(1) - 1)
    def _():
        rec_ref[...] = st_ref[...]


def _gdn_prompt_call(qkv, gz, gba, gbat, hrow, hcol, nw, ub, lb):
    b, s, _ = qkv.shape
    ns = s // GDN_TS
    const2 = lambda shape: pl.BlockSpec(shape, lambda bi, i: (0, 0))
    return pl.pallas_call(
        _gdn_prompt_kernel,
        grid=(b, ns),
        in_specs=[
            pl.BlockSpec((None, GDN_TS, GDN_CONV_DIM), lambda bi, i: (bi, i, 0)),
            pl.BlockSpec((None, GDN_TS, GDN_WIDTH), lambda bi, i: (bi, i, 0)),
            pl.BlockSpec((None, GDN_TS, LANES), lambda bi, i: (bi, i, 0)),
            pl.BlockSpec((None, SUBLANES, GDN_TS), lambda bi, i: (bi, 0, i)),
            const2((2, LANES)), const2((SUBLANES, 2)),
            const2((1, GDN_HEAD_DIM)), const2((GDN_TS, GDN_TS)), const2((GDN_TS, GDN_TS)),
        ],
        out_specs=[
            pl.BlockSpec((None, GDN_TS, GDN_WIDTH), lambda bi, i: (bi, i, 0)),
            pl.BlockSpec((None, GDN_HEADS, GDN_HEAD_DIM, GDN_HEAD_DIM), lambda bi, i: (bi, 0, 0, 0)),
        ],
        out_shape=[
            jax.ShapeDtypeStruct((b, s, GDN_WIDTH), BF16),
            jax.ShapeDtypeStruct((b, GDN_HEADS, GDN_HEAD_DIM, GDN_HEAD_DIM), F32),
        ],
        scratch_shapes=[pltpu.VMEM((GDN_HEADS, GDN_HEAD_DIM, GDN_HEAD_DIM), F32)],
        compiler_params=_params(("arbitrary", "arbitrary")),
        name="gdn_prompt",
    )(qkv, gz, gba, gbat, hrow, hcol, nw, ub, lb)


def _outproj_kernel(ysb_ref, ygdn_ref, x_ref, gate_ref, wt_ref, wb_ref, g_ref, b_ref, o_ref):
    mixed = (jnp.dot(ysb_ref[...], wt_ref[...], preferred_element_type=F32)
             + jnp.dot(ygdn_ref[...], wb_ref[...], preferred_element_type=F32))
    r = DN_ALPHA * x_ref[...] + gate_ref[...] * mixed
    mu = jnp.mean(r, axis=-1, keepdims=True)
    rc = r - mu
    var = jnp.mean(rc * rc, axis=-1, keepdims=True)
    o_ref[...] = rc * lax.rsqrt(var + LN_EPS) * g_ref[...] + b_ref[...]


def _outproj_call(y_sb, y_gdn, x, gate, w_top, w_bot, ln_g, ln_b, tm, per_row_mod):
    g, t, _ = x.shape
    nt = t // tm
    if per_row_mod:
        gate_spec = pl.BlockSpec((None, tm, D_MODEL), lambda b, i: (b, i, 0))
    else:
        gate_spec = pl.BlockSpec((None, 1, D_MODEL), lambda b, i: (b, 0, 0))
    half = pl.BlockSpec((None, tm, SB_WIDTH), lambda b, i: (b, i, 0))
    full = pl.BlockSpec((None, tm, D_MODEL), lambda b, i: (b, i, 0))
    const = lambda shape: pl.BlockSpec(shape, lambda b, i: (0, 0))
    return pl.pallas_call(
        _outproj_kernel,
        grid=(g, nt),
        in_specs=[half, half, full, gate_spec, const((SB_WIDTH, D_MODEL)), const((GDN_WIDTH, D_MODEL)),
                  const((1, D_MODEL)), const((1, D_MODEL))],
        out_specs=full,
        out_shape=jax.ShapeDtypeStruct((g, t, D_MODEL), F32),
        compiler_params=_params(("arbitrary", "arbitrary")),
        name="out_proj_ln",
    )(y_sb, y_gdn, x, gate, w_top, w_bot, ln_g, ln_b)


def _gdn_step_kernel(qkv_ref, cs_ref, gz_ref, gba_ref, rec_ref, cw_ref, hrow_ref, nw_ref,
                     y_ref, cs_out_ref, rec_out_ref):
    cdim = GDN_CONV_DIM
    x = qkv_ref[...]
    y = x * cw_ref[CONV_W - 1:CONV_W, :]
    for i in range(CONV_W - 1):
        y = y + cs_ref[:, i * cdim:(i + 1) * cdim] * cw_ref[i:i + 1, :]
    qkv = _silu(y)
    for i in range(CONV_W - 2):
        cs_out_ref[:, i * cdim:(i + 1) * cdim] = cs_ref[:, (i + 1) * cdim:(i + 2) * cdim]
    cs_out_ref[:, (CONV_W - 2) * cdim:(CONV_W - 1) * cdim] = x

    gba = gba_ref[...]
    beta = jax.nn.sigmoid(gba)
    g = -jnp.exp(hrow_ref[0:1, :]) * _softplus(gba + hrow_ref[1:2, :])
    eye = (lax.broadcasted_iota(jnp.int32, (GDN_HEAD_DIM, GDN_HEAD_DIM), 0)
           == lax.broadcasted_iota(jnp.int32, (GDN_HEAD_DIM, GDN_HEAD_DIM), 1))

    def column(row):
        return jnp.sum(jnp.where(eye, row, 0.0), axis=1, keepdims=True)

    nw = nw_ref[...]
    gz = gz_ref[...]
    for h in range(GDN_HEADS):
        lo = h * GDN_HEAD_DIM
        hi = lo + GDN_HEAD_DIM
        q_h = _l2norm(qkv[:, lo:hi]) * GDN_HEAD_DIM ** -0.5
        k_h = _l2norm(qkv[:, GDN_WIDTH + lo:GDN_WIDTH + hi])
        v_h = qkv[:, 2 * GDN_WIDTH + lo:2 * GDN_WIDTH + hi]
        s = rec_ref[h] * jnp.exp(g[:, GDN_HEADS + h:GDN_HEADS + h + 1])
        kcol = column(k_h)
        delta = (v_h - jnp.sum(s * kcol, axis=0, keepdims=True)) * beta[:, h:h + 1]
        s = s + kcol * delta
        rec_out_ref[h] = s
        o = jnp.sum(s * column(q_h), axis=0, keepdims=True)
        y_ref[:, lo:hi] = _gdn_out(o, gz[:, lo:hi], nw).astype(BF16)


def _gdn_step_call(layer, qkv, conv_state, gz, gba, state_rec, cw, hrow, nw):
    db = qkv.shape[0]
    row = lambda width: pl.BlockSpec((None, 1, width), lambda b: (b, 0, 0))
    const = lambda shape: pl.BlockSpec(shape, lambda b: (0, 0))
    rec_shape = (GDN_HEADS, GDN_HEAD_DIM, GDN_HEAD_DIM)
    cs_width = (CONV_W - 1) * GDN_CONV_DIM
    return pl.pallas_call(
        _gdn_step_kernel,
        grid=(db,),
        in_specs=[row(GDN_CONV_DIM), row(cs_width), row(GDN_WIDTH), row(LANES),
                  pl.BlockSpec((None, None) + rec_shape, lambda b: (layer, b, 0, 0, 0)),
                  const((CONV_W, GDN_CONV_DIM)), const((2, LANES)), const((1, GDN_HEAD_DIM))],
        out_specs=[row(GDN_WIDTH), row(cs_width),
                   pl.BlockSpec((None,) + rec_shape, lambda b: (b, 0, 0, 0))],
        out_shape=[jax.ShapeDtypeStruct((db, 1, GDN_WIDTH), BF16),
                   jax.ShapeDtypeStruct((db, 1, cs_width), F32),
                   jax.ShapeDtypeStruct((db,) + rec_shape, F32)],
        compiler_params=_params(("arbitrary",)),
        name="gdn_step",
    )(qkv, conv_state, gz, gba, state_rec, cw, hrow, nw)


PAGES_PER_STEP = 32


def _sb_paged_kernel(pt_ref, q_ref, kn_ref, vn_ref, sg_ref, bias_ref, uu_ref, *rest):
    del pt_ref
    np_ = PAGES_PER_STEP
    k_refs = rest[:np_]
    v_refs = rest[np_:2 * np_]
    y_ref, qm_ref, carry_ref, acc_ref = rest[2 * np_:]
    gi = pl.program_id(1)
    head = lax.broadcasted_iota(jnp.int32, (SB_HEADS, SB_WIDTH), 0)
    chan = lax.broadcasted_iota(jnp.int32, (SB_HEADS, SB_WIDTH), 1)
    own = chan // SB_HEAD_DIM == head

    @pl.when(gi == 0)
    def _():
        qm_ref[...] = jnp.where(own, q_ref[...].astype(F32), 0.0)
        carry_ref[...] = jnp.zeros_like(carry_ref)
        acc_ref[...] = jnp.zeros_like(acc_ref)

    qm32 = qm_ref[...]
    qm = qm32.astype(BF16)
    bias = bias_ref[...] * LOG2E
    tk = SB_TK
    kt_all = jnp.concatenate([r[...].astype(BF16) for r in k_refs], axis=1)
    z = jnp.dot(qm, kt_all, preferred_element_type=F32) + bias
    sp = _softplus2(z)
    hi, lo = _split_bf16(jnp.concatenate([sp[:, i * tk:(i + 1) * tk] for i in range(np_)], axis=0))
    cs = jnp.dot(jnp.concatenate([hi, lo], axis=1), uu_ref[...], preferred_element_type=F32)
    carry = carry_ref[...]
    ws = [None] * np_
    for i in reversed(range(np_)):
        cs_i = cs[i * SB_HEADS:(i + 1) * SB_HEADS]
        ws[i] = jnp.exp2(z[:, i * tk:(i + 1) * tk] - carry - cs_i[:, :tk])
        carry = carry + cs_i[:, tk:]
    carry_ref[...] = carry
    vt_all = jnp.concatenate([r[...].astype(BF16) for r in v_refs], axis=1)
    acc = acc_ref[...] + lax.dot_general(jnp.concatenate(ws, axis=1).astype(BF16), vt_all, _NT,
                                         preferred_element_type=F32)
    acc_ref[...] = acc

    @pl.when(gi == pl.num_programs(1) - 1)
    def _():
        t_new = kn_ref.shape[0]
        z_new = jnp.sum(qm32 * kn_ref[...], axis=-1, keepdims=True) + bias
        q_idx = lax.broadcasted_iota(jnp.int32, (SB_HEADS, t_new), 1) + (t_new - 1)
        k_idx = lax.broadcasted_iota(jnp.int32, (SB_HEADS, t_new), 1)
        w_new = jnp.where(k_idx < q_idx, jnp.exp2(z_new - _softplus(z_new / LOG2E) * LOG2E), 0.0)
        total = acc + w_new * vn_ref[...]
        o = jnp.sum(jnp.where(own, total, 0.0), axis=0, keepdims=True)
        y_ref[...] = (o * _silu(sg_ref[...])).astype(BF16)


def _sb_paged_call(layer, page_table, q_bf, k_new, v_new, sg, bias_col, uu, cache_k, cache_v):
    db, n_pages = page_table.shape
    np_ = PAGES_PER_STEP
    ng = n_pages // np_
    page = cache_k.shape[2]
    ck = jnp.transpose(cache_k, (0, 1, 3, 4, 2)).reshape(cache_k.shape[0], cache_k.shape[1], SB_WIDTH, page)
    cv = jnp.transpose(cache_v, (0, 1, 3, 4, 2)).reshape(cache_v.shape[0], cache_v.shape[1], SB_WIDTH, page)
    row = pl.BlockSpec((None, 1, SB_WIDTH), lambda b, g, pt: (b, 0, 0))

    def page_spec(i):
        return pl.BlockSpec(
            (None, None, SB_WIDTH, page),
            lambda b, g, pt: (layer, pt[b, (ng - 1 - g) * np_ + i], 0, 0))

    grid_spec = pltpu.PrefetchScalarGridSpec(
        num_scalar_prefetch=1,
        grid=(db, ng),
        in_specs=[row, row, row, row,
                  pl.BlockSpec((SB_HEADS, 1), lambda b, g, pt: (0, 0)),
                  pl.BlockSpec((2 * SB_TK, 2 * SB_TK), lambda b, g, pt: (0, 0))]
                 + [page_spec(i) for i in range(np_)] * 2,
        out_specs=row,
        scratch_shapes=[pltpu.VMEM((SB_HEADS, SB_WIDTH), F32),
                        pltpu.VMEM((SB_HEADS, SB_TK), F32),
                        pltpu.VMEM((SB_HEADS, SB_WIDTH), F32)],
    )
    return pl.pallas_call(
        _sb_paged_kernel,
        grid_spec=grid_spec,
        out_shape=jax.ShapeDtypeStruct((db, 1, SB_WIDTH), BF16),
        compiler_params=_params(("arbitrary", "arbitrary")),
        name="sb_paged",
    )(page_table, q_bf, k_new, v_new, sg, bias_col, uu, *([ck] * np_), *([cv] * np_))


def _pick_tile(t, cap):
    tm = min(t, cap)
    assert t % tm == 0
    return tm


def kernel(x_prompt, x_sample, cache_k, cache_v, page_table, state_conv, state_rec, c_prompt, c_sample,
           w_mod, b_mod, w_in, sb_bias, conv_w, a_log, dt_bias, gdn_norm_w, w_out, ln_g, ln_b):
    b, s, _ = x_prompt.shape
    db, t_new, _ = x_sample.shape
    assert t_new == 1 and s % GDN_TS == 0 and s % SB_TQ == 0
    assert page_table.shape[1] % PAGES_PER_STEP == 0 and cache_k.shape[2] == SB_TK

    mod = _mod_call(jnp.concatenate([c_sample, c_prompt], axis=0), w_mod, b_mod)
    mod_rows = mod.reshape(DEPTH, 3, db + b, 1, D_MODEL)

    uu = _cumsum_matrix()
    ub = _chunk_cumsum_matrix(GDN_TS)
    lb = ub.T
    gate_pad = ((0, 0), (GDN_HEADS, LANES - 2 * GDN_HEADS))

    hp, hs = x_prompt, x_sample.reshape(1, db, D_MODEL)
    cp_l, rp_l, ks_l, vs_l, cs_l, rs_l = [], [], [], [], [], []
    kv_prompt = tuple(jnp.zeros((DEPTH, b, SB_WIDTH, s), F32) for _ in range(2))
    for l in range(DEPTH):
        w_bf = jnp.pad(w_in[l], ((0, 0), (0, W_IN_PAD - w_in.shape[2]))).astype(BF16)
        wt_bf = jnp.concatenate(
            [w_in[l][:, COL_K:COL_G], w_in[l][:, COL_BA:COL_BA + SUBLANES]], axis=1).T.astype(BF16)
        w_top = w_out[l][:SB_WIDTH].astype(BF16)
        w_bot = w_out[l][SB_WIDTH:].astype(BF16)
        hrow = jnp.pad(jnp.stack([a_log[l], dt_bias[l]]), gate_pad)
        hcol = hrow[:, :SUBLANES].T
        nw = gdn_norm_w[l].reshape(1, GDN_HEAD_DIM)
        lng = ln_g[l].reshape(1, D_MODEL)
        lnb = ln_b[l].reshape(1, D_MODEL)

        shift, scale, gate = (mod_rows[l, i, db:] for i in range(3))
        tm = _pick_tile(s, 256)
        q_bf, kt, vt, kt_bf, vt_bf, sg, qkv, gz, gba, gbat, conv_p = _inproj_call(
            hp, shift, scale, w_bf, wt_bf, tm, per_row_mod=False, transposed_kv=True,
            layer=l, kv_buffers=kv_prompt, conv_w=conv_w[l])
        kv_prompt = (kt, vt)
        y_sb = _sb_prompt_call(q_bf, kt_bf, vt_bf, sg, sb_bias[l], uu[:SB_TK, :SB_TK])
        y_gdn, rec_p = _gdn_prompt_call(qkv, gz, gba, gbat, hrow, hcol, nw, ub, lb)
        hp = _outproj_call(y_sb, y_gdn, hp, gate, w_top, w_bot, lng, lnb, _pick_tile(s, 512), False)
        cp_l.append(conv_p)
        rp_l.append(rec_p)

        shift, scale, gate = (mod[l, i, :db].reshape(1, db, D_MODEL) for i in range(3))
        q_bf, k, v, sg, qkv, gz, gba = _inproj_call(
            hs, shift, scale, w_bf, wt_bf, db, per_row_mod=True, transposed_kv=False)
        as_rows = lambda a: a.reshape(db, 1, a.shape[-1])
        y_sb = _sb_paged_call(l, page_table, as_rows(q_bf), as_rows(k), as_rows(v), as_rows(sg),
                              sb_bias[l].reshape(SB_HEADS, 1), uu, cache_k, cache_v)
        y_gdn, conv_s, rec_s = _gdn_step_call(
            l, as_rows(qkv), state_conv[l].reshape(db, 1, (CONV_W - 1) * GDN_CONV_DIM),
            as_rows(gz), as_rows(gba), state_rec, conv_w[l], hrow, nw)
        hs = _outproj_call(y_sb.reshape(1, db, SB_WIDTH), y_gdn.reshape(1, db, GDN_WIDTH), hs, gate,
                           w_top, w_bot, lng, lnb, db, True)
        ks_l.append(k.reshape(db, 1, SB_HEADS, SB_HEAD_DIM))
        vs_l.append(v.reshape(db, 1, SB_HEADS, SB_HEAD_DIM))
        cs_l.append(conv_s.reshape(db, CONV_W - 1, GDN_CONV_DIM))
        rs_l.append(rec_s)

    heads_last = lambda a: jnp.transpose(a.reshape(DEPTH, b, SB_HEADS, SB_HEAD_DIM, s), (0, 1, 4, 2, 3))
    k_prompt, v_prompt = (heads_last(a) for a in kv_prompt)
    return (hp, hs.reshape(db, 1, D_MODEL), k_prompt, v_prompt, jnp.stack(cp_l),
            jnp.stack(rp_l), jnp.stack(ks_l), jnp.stack(vs_l), jnp.stack(cs_l), jnp.stack(rs_l))
```
